```python
import math
import jax
import jax.numpy as jnp
from jax import lax
import numpy as np

D_MODEL = 1024
BATCH = 4
SEQ = 8192
DEPTH = 1
DEC_BATCH = 128
DEC_SEQ = 8
PAST_LEN = 8192
PAGE_SIZE = 128

N_HEADS_A = 8
HEAD_DIM_A = 64
WIDTH_A = N_HEADS_A * HEAD_DIM_A
QKV_A = 3 * WIDTH_A
CONV_WIDTH = 4
DELTA_CHUNK = 64
N_HEADS_B = 8
HEAD_DIM_B = 64
WIDTH_B = N_HEADS_B * HEAD_DIM_B
MOBA_BLOCK = 256
MOBA_TOP = 3
Q_ROWS = 512
ATTN_SCALE = HEAD_DIM_B ** -0.5
NUM_BUCKETS = 32
MAX_DISTANCE = 128
IN_COLS = QKV_A + WIDTH_A + 2 * N_HEADS_A + 3 * WIDTH_B
N_EXPERTS = 32
TOP_K = 4
D_FF = 1024
SWIGLU_LIMIT = 7.0
SWIGLU_ALPHA = 1.702
ROW_BLOCK = 256
RMS_EPS = 1e-6

kernel_name = 'hybrid_gdn_moba_moe_step'


def rms_norm(x, g):
    xf = x.astype(jnp.float32)
    y = xf * lax.rsqrt(jnp.mean(xf * xf, axis=-1, keepdims=True) + RMS_EPS)
    return (y * g.astype(jnp.float32)).astype(x.dtype)


def l2_norm(x):
    xf = x.astype(jnp.float32)
    return xf * lax.rsqrt(jnp.sum(xf * xf, axis=-1, keepdims=True) + RMS_EPS)


def causal_conv(u, buf, w):
    T = u.shape[1]
    u_ext = jnp.concatenate([buf.astype(u.dtype), u], axis=1)
    out = sum(w[i].astype(u.dtype) * u_ext[:, i:i + T] for i in range(CONV_WIDTH))
    return out, u_ext[:, T:]


def gated_delta_rule(q, k, v, g, beta, S0):
    B, T, H, DK = q.shape
    DV = v.shape[-1]
    C = min(DELTA_CHUNK, T)
    nc = -(-T // C)
    pad = nc * C - T

    def to_chunks(a):
        a = jnp.pad(a.astype(jnp.float32), [(0, 0), (0, pad)] + [(0, 0)] * (a.ndim - 2))
        a = jnp.moveaxis(a, 2, 1)
        return a.reshape(a.shape[:2] + (nc, C) + a.shape[3:])

    qc, kc, vc, gc, bc = (to_chunks(a) for a in (q, k, v, g, beta))
    G = jnp.cumsum(gc, axis=-1)
    tri = jnp.tril(jnp.ones((C, C), bool))
    strict = jnp.tril(jnp.ones((C, C), bool), -1)
    diff = G[..., :, None] - G[..., None, :]
    decay = jnp.where(tri, jnp.exp(jnp.where(tri, diff, 0.0)), 0.0)
    kb = kc * bc[..., None]
    L = jnp.where(strict, jnp.einsum('bhncd,bhnjd->bhncj', kb, kc) * decay, 0.0)
    eye = jnp.eye(C, dtype=jnp.float32)
    Tinv = lax.linalg.triangular_solve(eye + L, jnp.broadcast_to(eye, L.shape),
                                       left_side=True, lower=True, unit_diagonal=True)
    value = jnp.einsum('bhncj,bhnje->bhnce', Tinv, vc * bc[..., None])
    kcum = jnp.einsum('bhncj,bhnjd->bhncd', Tinv, kb * jnp.exp(G)[..., None])
    attn = jnp.einsum('bhncd,bhnjd->bhncj', qc, kc) * decay
    q_dec = qc * jnp.exp(G)[..., None]
    k_dec = kc * jnp.exp(G[..., -1:] - G)[..., None]
    g_tot = jnp.exp(G[..., -1])

    def step(S, xs):
        val, kcu, at, qd, kd, gt = xs
        u = val - jnp.einsum('bhcd,bhde->bhce', kcu, S)
        o = jnp.einsum('bhcd,bhde->bhce', qd, S) + jnp.einsum('bhcj,bhje->bhce', at, u)
        S = S * gt[..., None, None] + jnp.einsum('bhcd,bhce->bhde', kd, u)
        return S, o

    xs = tuple(jnp.moveaxis(a, 2, 0) for a in (value, kcum, attn, q_dec, k_dec, g_tot))
    S_fin, o = lax.scan(step, S0.astype(jnp.float32), xs)
    o = jnp.moveaxis(o, 0, 2).reshape(B, H, nc * C, DV)[:, :, :T]
    return jnp.moveaxis(o, 1, 2), S_fin


def t5_bucket(dist):
    n = jnp.maximum(dist, 0)
    max_exact = NUM_BUCKETS // 2
    n_f = jnp.maximum(n, max_exact).astype(jnp.float32)
    large = max_exact + (jnp.log(n_f / max_exact) / math.log(MAX_DISTANCE / max_exact)
                         * (NUM_BUCKETS - max_exact)).astype(jnp.int32)
    large = jnp.minimum(large, NUM_BUCKETS - 1)
    return jnp.where(n < max_exact, n, large)


def assemble_blocks(new_rows, cache, page_table, layer_idx):
    B, T, H, D = new_rows.shape
    parts = []
    past_len = 0
    if cache is not None:
        past_len = page_table.shape[1] * cache.shape[2]
        parts.append(cache[layer_idx, page_table].reshape(B, past_len, H, D).astype(new_rows.dtype))
    L = past_len + T
    nb = -(-L // MOBA_BLOCK)
    parts.append(new_rows)
    parts.append(jnp.zeros((B, nb * MOBA_BLOCK - L, H, D), new_rows.dtype))
    return jnp.concatenate(parts, axis=1).reshape(B, nb, MOBA_BLOCK, H, D), past_len


def moba_attention(q, k_blocks, v_blocks, rel_bias, q_start):
    B, Tq, H, D = q.shape
    nb = k_blocks.shape[1]
    k_means = jnp.mean(k_blocks.astype(jnp.float32), axis=2)
    qc = max(1, min(Tq, Q_ROWS // B))
    nq = -(-Tq // qc)
    q_pad = jnp.pad(q, ((0, 0), (0, nq * qc - Tq), (0, 0), (0, 0)))
    q_chunks = q_pad.reshape(B, nq, qc, H, D).transpose(1, 0, 3, 2, 4)
    pos_chunks = (q_start + jnp.arange(nq * qc, dtype=jnp.int32)).reshape(nq, qc)
    bias_table = rel_bias.astype(jnp.float32).T
    bi = jnp.arange(B)[:, None, None, None]
    hi = jnp.arange(H)[None, :, None, None]
    slot = jnp.arange(MOBA_TOP + 1)[None, None, None, :, None]
    offs = jnp.arange(MOBA_BLOCK, dtype=jnp.int32)
    n_gate = max(nb, MOBA_TOP)

    def attend(args):
        qb, pos = args
        qf = qb.astype(jnp.float32)
        own = pos // MOBA_BLOCK
        gate = jnp.einsum('bhqd,bnhd->bhqn', qf, k_means)
        past = jnp.arange(nb)[None, :] < own[:, None]
        gate = jnp.where(past, gate, -jnp.inf)
        if n_gate > nb:
            gate = jnp.pad(gate, ((0, 0), (0, 0), (0, 0), (0, n_gate - nb)), constant_values=-jnp.inf)
        top = jnp.minimum(lax.top_k(gate, MOBA_TOP)[1], nb - 1)
        own_b = jnp.broadcast_to(jnp.minimum(own, nb - 1)[None, None, :, None], (B, H, qc, 1))
        sel = jnp.concatenate([top, own_b], axis=-1).astype(jnp.int32)
        ks = k_blocks[bi, sel, :, hi]
        vs = v_blocks[bi, sel, :, hi]
        key_pos = sel[..., None] * MOBA_BLOCK + offs
        qpos = pos[None, None, :, None, None]
        valid = jnp.where(slot == MOBA_TOP, key_pos <= qpos, slot < own[None, None, :, None, None])
        bias = bias_table[hi[..., None], t5_bucket(qpos - key_pos)]
        logits = jnp.einsum('bhqd,bhqskd->bhqsk', qf, ks.astype(jnp.float32)) * ATTN_SCALE + bias
        logits = jnp.where(valid, logits, -jnp.inf)
        p = jax.nn.softmax(logits.reshape(B, H, qc, -1), axis=-1).reshape(logits.shape)
        return jnp.einsum('bhqsk,bhqskd->bhqd', p, vs.astype(jnp.float32))

    out = lax.map(attend, (q_chunks, pos_chunks))
    return out.transpose(1, 0, 3, 2, 4).reshape(B, nq * qc, H, D)[:, :Tq].astype(q.dtype)


def moe_ffn(x2d, w_router, b_router, w_gate, b_gate, w_up, b_up, w_down, b_down):
    N, D = x2d.shape
    logits = jnp.dot(x2d, w_router).astype(jnp.float32) + b_router.astype(jnp.float32)
    top_vals, top_e = lax.top_k(logits, TOP_K)
    gates = jax.nn.softmax(top_vals, axis=-1)
    NK = N * TOP_K
    flat_e = top_e.reshape(NK)
    flat_tok = jnp.arange(NK, dtype=jnp.int32) // TOP_K
    flat_gate = gates.reshape(NK)
    order = jnp.argsort(flat_e)
    sorted_e = flat_e[order]
    counts = jnp.bincount(flat_e, length=N_EXPERTS)
    padded = ((counts + ROW_BLOCK - 1) // ROW_BLOCK) * ROW_BLOCK
    pad_end = jnp.cumsum(padded)
    pad_start = pad_end - padded
    start = jnp.cumsum(counts) - counts
    dest = pad_start[sorted_e] + (jnp.arange(NK) - start[sorted_e])
    n_blocks = -(-NK // ROW_BLOCK) + N_EXPERTS
    n_rows = n_blocks * ROW_BLOCK
    slot_tok = jnp.full((n_rows,), N, jnp.int32).at[dest].set(flat_tok[order])
    slot_gate = jnp.zeros((n_rows,), jnp.float32).at[dest].set(flat_gate[order])
    block_e = jnp.minimum(jnp.searchsorted(pad_end, jnp.arange(n_blocks) * ROW_BLOCK, side='right'),
                          N_EXPERTS - 1)
    x_ext = jnp.concatenate([x2d, jnp.zeros((1, D), x2d.dtype)], axis=0)
    xb = x_ext[slot_tok].reshape(n_blocks, ROW_BLOCK, D)

    def expert_block(args):
        xr, e = args
        gate = jnp.minimum(xr @ w_gate[e] + b_gate[e], SWIGLU_LIMIT)
        up = jnp.clip(xr @ w_up[e] + b_up[e], -SWIGLU_LIMIT, SWIGLU_LIMIT)
        glu = gate * jax.nn.sigmoid(SWIGLU_ALPHA * gate)
        return ((up + 1.0) * glu) @ w_down[e] + b_down[e]

    yb = lax.map(expert_block, (xb, block_e)).reshape(n_rows, D)
    y = jax.ops.segment_sum(yb.astype(jnp.float32) * slot_gate[:, None], slot_tok, num_segments=N + 1)
    return y[:N].astype(x2d.dtype)


def layer(x, cache_k, cache_v, page_table, layer_idx, S0, conv_buf, rel_bias,
          norm1_g, w_in, conv_w, A_log, dt_bias, o_norm_g, q_norm_g, k_norm_g, w_out,
          norm2_g, w_router, b_router, w_gate, b_gate, w_up, b_up, w_down, b_down):
    B, T, D = x.shape
    h = rms_norm(x, norm1_g)
    proj = h @ w_in
    c0 = QKV_A
    c1 = c0 + WIDTH_A
    c2 = c1 + N_HEADS_A
    c3 = c2 + N_HEADS_A
    qkv_a, new_conv = causal_conv(proj[..., :c0], conv_buf, conv_w)
    qkv_a = jax.nn.silu(qkv_a).reshape(B, T, 3, N_HEADS_A, HEAD_DIM_A)
    q_a = l2_norm(qkv_a[:, :, 0]) * HEAD_DIM_A ** -0.5
    k_a = l2_norm(qkv_a[:, :, 1])
    v_a = qkv_a[:, :, 2]
    z = proj[..., c0:c1].reshape(B, T, N_HEADS_A, HEAD_DIM_A)
    beta = jax.nn.sigmoid(proj[..., c1:c2].astype(jnp.float32))
    g = -jnp.exp(A_log.astype(jnp.float32)) * jax.nn.softplus(
        proj[..., c2:c3].astype(jnp.float32) + dt_bias.astype(jnp.float32))
    o_a, S_new = gated_delta_rule(q_a, k_a, v_a, g, beta, S0)
    o_a = rms_norm(o_a, o_norm_g) * jax.nn.silu(z.astype(jnp.float32))
    qkv_b = proj[..., c3:].reshape(B, T, 3, N_HEADS_B, HEAD_DIM_B)
    q_b = rms_norm(qkv_b[:, :, 0], q_norm_g)
    k_b = rms_norm(qkv_b[:, :, 1], k_norm_g)
    v_b = qkv_b[:, :, 2]
    k_blocks, past_len = assemble_blocks(k_b, cache_k, page_table, layer_idx)
    v_blocks, _ = assemble_blocks(v_b, cache_v, page_table, layer_idx)
    o_b = moba_attention(q_b, k_blocks, v_blocks, rel_bias, past_len)
    mixed = jnp.concatenate([o_a.reshape(B, T, WIDTH_A).astype(x.dtype),
                             o_b.reshape(B, T, WIDTH_B).astype(x.dtype)], axis=-1) @ w_out
    x = x + mixed
    ffn = moe_ffn(rms_norm(x, norm2_g).reshape(B * T, D), w_router, b_router,
                  w_gate, b_gate, w_up, b_up, w_down, b_down).reshape(B, T, D)
    return x + ffn, k_b, v_b, S_new.astype(S0.dtype), new_conv


def setup_inputs(seed: int = 0) -> dict:
    key = jax.random.key(seed)
    ks = jax.random.split(key, 28)
    f32 = jnp.float32
    n_pages = PAST_LEN // PAGE_SIZE
    n_used = DEC_BATCH * n_pages
    n_pool = n_used + n_used // 4

    def nrm(k, shape, scale):
        return scale * jax.random.normal(k, shape, f32)

    x_prompt = nrm(ks[0], (BATCH, SEQ, D_MODEL), 1.0)
    x_sample = nrm(ks[1], (DEC_BATCH, DEC_SEQ, D_MODEL), 1.0)
    cache_k = nrm(ks[2], (DEPTH, n_pool, PAGE_SIZE, N_HEADS_B, HEAD_DIM_B), 1.0)
    cache_v = nrm(ks[3], (DEPTH, n_pool, PAGE_SIZE, N_HEADS_B, HEAD_DIM_B), 1.0)
    state_delta = nrm(ks[4], (DEPTH, DEC_BATCH, N_HEADS_A, HEAD_DIM_A, HEAD_DIM_A), 0.1)
    state_conv = nrm(ks[5], (DEPTH, DEC_BATCH, CONV_WIDTH - 1, QKV_A), 1.0)
    page_table = jax.random.permutation(ks[6], n_pool)[:n_used].reshape(DEC_BATCH, n_pages).astype(jnp.int32)
    rel_bias = nrm(ks[7], (NUM_BUCKETS, N_HEADS_B), 0.1)
    norm1_g = 1.0 + nrm(ks[8], (DEPTH, D_MODEL), 0.02)
    w_in = nrm(ks[9], (DEPTH, D_MODEL, IN_COLS), D_MODEL ** -0.5)
    conv_w = nrm(ks[10], (DEPTH, CONV_WIDTH, QKV_A), 0.5)
    A_log = jnp.log(jax.random.uniform(ks[11], (DEPTH, N_HEADS_A), f32, 1.0, 16.0))
    dt = jnp.exp(jax.random.uniform(ks[12], (DEPTH, N_HEADS_A), f32, math.log(1e-3), math.log(1e-1)))
    dt_bias = dt + jnp.log(-jnp.expm1(-dt))
    o_norm_g = 1.0 + nrm(ks[13], (DEPTH, HEAD_DIM_A), 0.02)
    q_norm_g = 1.0 + nrm(ks[14], (DEPTH, HEAD_DIM_B), 0.02)
    k_norm_g = 1.0 + nrm(ks[15], (DEPTH, HEAD_DIM_B), 0.02)
    w_out = nrm(ks[16], (DEPTH, WIDTH_A + WIDTH_B, D_MODEL), (WIDTH_A + WIDTH_B) ** -0.5)
    norm2_g = 1.0 + nrm(ks[17], (DEPTH, D_MODEL), 0.02)
    w_router = nrm(ks[18], (DEPTH, D_MODEL, N_EXPERTS), D_MODEL ** -0.5)
    b_router = nrm(ks[19], (DEPTH, N_EXPERTS), 0.01)
    w_gate = nrm(ks[20], (DEPTH, N_EXPERTS, D_MODEL, D_FF), D_MODEL ** -0.5)
    b_gate = nrm(ks[21], (DEPTH, N_EXPERTS, D_FF), 0.02)
    w_up = nrm(ks[22], (DEPTH, N_EXPERTS, D_MODEL, D_FF), D_MODEL ** -0.5)
    b_up = nrm(ks[23], (DEPTH, N_EXPERTS, D_FF), 0.02)
    w_down = nrm(ks[24], (DEPTH, N_EXPERTS, D_FF, D_MODEL), D_FF ** -0.5)
    b_down = nrm(ks[25], (DEPTH, N_EXPERTS, D_MODEL), 0.02)
    return {'x_prompt': x_prompt, 'x_sample': x_sample, 'cache_k': cache_k, 'cache_v': cache_v,
            'state_delta': state_delta, 'state_conv': state_conv, 'page_table': page_table,
            'rel_bias': rel_bias, 'norm1_g': norm1_g, 'w_in': w_in, 'conv_w': conv_w, 'A_log': A_log,
            'dt_bias': dt_bias, 'o_norm_g': o_norm_g, 'q_norm_g': q_norm_g, 'k_norm_g': k_norm_g,
            'w_out': w_out, 'norm2_g': norm2_g, 'w_router': w_router, 'b_router': b_router,
            'w_gate': w_gate, 'b_gate': b_gate, 'w_up': w_up, 'b_up': b_up, 'w_down': w_down,
            'b_down': b_down}


def reference(x_prompt, x_sample, cache_k, cache_v, state_delta, state_conv, page_table, rel_bias,
              norm1_g, w_in, conv_w, A_log, dt_bias, o_norm_g, q_norm_g, k_norm_g, w_out, norm2_g,
              w_router, b_router, w_gate, b_gate, w_up, b_up, w_down, b_down):
    B = x_prompt.shape[0]
    yp = x_prompt
    ys = x_sample
    kp_l, vp_l, sp_l, cp_l = [], [], [], []
    ks_l, vs_l, ss_l, cs_l = [], [], [], []
    for l in range(DEPTH):
        lw = (norm1_g[l], w_in[l], conv_w[l], A_log[l], dt_bias[l], o_norm_g[l], q_norm_g[l],
              k_norm_g[l], w_out[l], norm2_g[l], w_router[l], b_router[l], w_gate[l], b_gate[l],
              w_up[l], b_up[l], w_down[l], b_down[l])
        s0_p = jnp.zeros((B, N_HEADS_A, HEAD_DIM_A, HEAD_DIM_A), x_prompt.dtype)
        c0_p = jnp.zeros((B, CONV_WIDTH - 1, QKV_A), x_prompt.dtype)
        yp, k_new, v_new, s_new, c_new = layer(yp, None, None, None, l, s0_p, c0_p, rel_bias, *lw)
        kp_l.append(k_new)
        vp_l.append(v_new)
        sp_l.append(s_new)
        cp_l.append(c_new)
        ys, k_new, v_new, s_new, c_new = layer(ys, cache_k, cache_v, page_table, l, state_delta[l],
                                               state_conv[l], rel_bias, *lw)
        ks_l.append(k_new)
        vs_l.append(v_new)
        ss_l.append(s_new)
        cs_l.append(c_new)
    y_prompt = yp
    y_sample = ys
    k_prompt = jnp.stack(kp_l)
    v_prompt = jnp.stack(vp_l)
    delta_prompt = jnp.stack(sp_l)
    conv_prompt = jnp.stack(cp_l)
    k_sample = jnp.stack(ks_l)
    v_sample = jnp.stack(vs_l)
    delta_sample = jnp.stack(ss_l)
    conv_sample = jnp.stack(cs_l)
    return (y_prompt, y_sample, k_prompt, v_prompt, delta_prompt, conv_prompt,
            k_sample, v_sample, delta_sample, conv_sample)
```

```python
import functools
import math

import numpy as np
import jax
import jax.numpy as jnp
from jax import lax
from jax.experimental import pallas as pl
from jax.experimental.pallas import tpu as pltpu

F32 = jnp.float32
BF16 = jnp.bfloat16

D_MODEL = 1024
N_HEADS = 8
HEAD_DIM = 64
WIDTH = N_HEADS * HEAD_DIM
QKV = 3 * WIDTH
CONV_WIDTH = 4
DELTA_CHUNK = 64
MOBA_BLOCK = 256
MOBA_TOP = 3
ATTN_SCALE = HEAD_DIM ** -0.5
NUM_BUCKETS = 32
MAX_DISTANCE = 128
N_EXPERTS = 32
TOP_K = 4
SWIGLU_LIMIT = 7.0
SWIGLU_ALPHA = 1.702
RMS_EPS = 1e-6

LANES = 128
ROW_TILE = 256
MOE_ROWS = 256
NEG = -1e30
VMEM_LIMIT = 48 * 1024 * 1024

_NN = (((1,), (0,)), ((), ()))
_NT = (((1,), (1,)), ((), ()))
_TN = (((0,), (0,)), ((), ()))


def _dot(a, b, dims=_NN):
    return lax.dot_general(a, b, dims, preferred_element_type=F32)


def _split2(a):
    hi = a.astype(BF16)
    lo = (a - hi.astype(F32)).astype(BF16)
    return hi, lo


def _split3(a):
    hi = a.astype(BF16)
    r = a - hi.astype(F32)
    mid = r.astype(BF16)
    lo = (r - mid.astype(F32)).astype(BF16)
    return hi, mid, lo


def _mm1(a, b, dims=_NN):
    return _dot(a.astype(BF16), b.astype(BF16), dims)


def _mm3(a, b, dims=_NN):
    ah, al = _split2(a)
    bh, bl = _split2(b)
    return _dot(ah, bh, dims) + (_dot(ah, bl, dims) + _dot(al, bh, dims))


def _mm_exact_rhs(a, b_bf16, dims=_NN):
    hi, mid, lo = _split3(a)
    return _dot(hi, b_bf16, dims) + (_dot(mid, b_bf16, dims) + _dot(lo, b_bf16, dims))


def _mm_exact_lhs(a_bf16, b, dims=_NN):
    hi, mid, lo = _split3(b)
    return _dot(a_bf16, hi, dims) + (_dot(a_bf16, mid, dims) + _dot(a_bf16, lo, dims))


def _seg_mean(x, bd):
    return _mm_exact_rhs(x, bd)


def _sigmoid(x):
    return 1.0 / (1.0 + jnp.exp(-x))


def _softplus(x):
    return jnp.maximum(x, 0.0) + jnp.log(1.0 + jnp.exp(-jnp.abs(x)))


def _inproj_body(x_ref, g_ref, w_ref, bd_ref, qg_ref, kg_ref,
                 qkva_ref, z_ref, bg_ref, qb_ref, kb_ref, vb_ref, km_ref):
    x = x_ref[...]
    ms = jnp.mean(x * x, axis=-1, keepdims=True)
    h = (x * lax.rsqrt(ms + RMS_EPS) * g_ref[...]).astype(BF16)

    def proj(lo, hi):
        return _dot(h, w_ref[:, lo:hi])

    bd = bd_ref[...]
    qkva_ref[...] = proj(0, QKV)
    z_ref[...] = proj(QKV, QKV + WIDTH)
    c = QKV + WIDTH
    q = proj(c, c + WIDTH)
    qb_ref[...] = q * lax.rsqrt(_seg_mean(q * q, bd) + RMS_EPS) * qg_ref[...]
    k = proj(c + WIDTH, c + 2 * WIDTH)
    kn = k * lax.rsqrt(_seg_mean(k * k, bd) + RMS_EPS) * kg_ref[...]
    kb_ref[...] = kn
    km_ref[0] = jnp.mean(kn, axis=0, keepdims=True)
    vb_ref[...] = proj(c + 2 * WIDTH, c + 3 * WIDTH)
    bg_ref[...] = proj(c + 3 * WIDTH, c + 3 * WIDTH + LANES)


def _inproj(x2d, g1, w_cat, bd, qg, kg):
    n = x2d.shape[0]
    tm = ROW_TILE
    nt = n // tm
    ncols = w_cat.shape[1]
    row = lambda w: pl.BlockSpec((tm, w), lambda i: (i, 0))
    full = lambda a: pl.BlockSpec(a.shape, lambda i: (0,) * a.ndim)
    outs = (
        jax.ShapeDtypeStruct((n, QKV), F32), jax.ShapeDtypeStruct((n, WIDTH), F32),
        jax.ShapeDtypeStruct((n, LANES), F32), jax.ShapeDtypeStruct((n, WIDTH), F32),
        jax.ShapeDtypeStruct((n, WIDTH), F32), jax.ShapeDtypeStruct((n, WIDTH), F32),
        jax.ShapeDtypeStruct((nt, 1, WIDTH), F32),
    )
    return pl.pallas_call(
        _inproj_body,
        grid=(nt,),
        in_specs=[row(D_MODEL), full(g1), full(w_cat), full(bd), full(qg), full(kg)],
        out_specs=(row(QKV), row(WIDTH), row(LANES), row(WIDTH), row(WIDTH), row(WIDTH),
                   pl.BlockSpec((1, 1, WIDTH), lambda i: (i, 0, 0))),
        out_shape=outs,
        compiler_params=pltpu.CompilerParams(dimension_semantics=("parallel",),
                                             vmem_limit_bytes=VMEM_LIMIT),
        name="inproj",
    )(x2d, g1, w_cat, bd, qg, kg)


GROUP = 4
GW = GROUP * HEAD_DIM


def _stack(x, masks):
    return jnp.concatenate([jnp.where(m, x, 0.0) for m in masks], axis=0)


def _unstack(w, c, n):
    acc = w[0:c]
    for h in range(1, n):
        acc = acc + w[h * c:(h + 1) * c]
    return acc


def _gdn_body(qkva_ref, z_ref, bg_ref, cbuf_ref, s0_ref, cw_ref, av_ref, dtb_ref, eb_ref, eg_ref,
              bd_ref, og_ref, o_ref, sfin_ref, s_scr, carry_scr, uext_scr, *, tt, c):
    t = pl.program_id(1)
    nt = pl.num_programs(1)

    @pl.when(t == 0)
    def _():
        s_scr[...] = s0_ref[0]
        carry_scr[...] = jnp.zeros_like(carry_scr)
        carry_scr[8 - (CONV_WIDTH - 1):8, :] = cbuf_ref[0]

    uext_scr[0:8, :] = carry_scr[...]
    uext_scr[8:8 + tt, :] = qkva_ref[0]
    carry_scr[...] = uext_scr[tt:tt + 8, :]

    cw = cw_ref[...]
    off = 8 - (CONV_WIDTH - 1)
    conv = cw[0:1] * uext_scr[off:off + tt, :]
    for i in range(1, CONV_WIDTH):
        conv = conv + cw[i:i + 1] * uext_scr[off + i:off + i + tt, :]
    act = conv * _sigmoid(conv)

    bd = bd_ref[...]
    q = act[:, 0:WIDTH]
    k = act[:, WIDTH:2 * WIDTH]
    v = act[:, 2 * WIDTH:3 * WIDTH]
    q = q * lax.rsqrt(_seg_mean(q * q, bd) * HEAD_DIM + RMS_EPS) * (HEAD_DIM ** -0.5)
    k = k * lax.rsqrt(_seg_mean(k * k, bd) * HEAD_DIM + RMS_EPS)

    bg = bg_ref[0]
    beta = _mm_exact_rhs(_sigmoid(bg), eb_ref[...])
    gx = _mm_exact_rhs(av_ref[...] * _softplus(bg + dtb_ref[...]), eg_ref[...])

    r4 = lax.broadcasted_iota(jnp.int32, (GROUP * c, GROUP * c), 0)
    c4 = lax.broadcasted_iota(jnp.int32, (GROUP * c, GROUP * c), 1)
    same = (r4 // c) == (c4 // c)
    tri = jnp.logical_and(same, r4 >= c4)
    strict = jnp.logical_and(same, r4 > c4)
    eye = (r4 == c4).astype(F32)
    rc = lax.broadcasted_iota(jnp.int32, (c, c), 0)
    cc = lax.broadcasted_iota(jnp.int32, (c, c), 1)
    tril_c = (rc >= cc).astype(BF16)
    lane_g = lax.broadcasted_iota(jnp.int32, (1, GW), 1) // HEAD_DIM
    hmasks = [lane_g == h for h in range(GROUP)]
    rs = lax.broadcasted_iota(jnp.int32, (GW, GW), 0) // HEAD_DIM
    cs = lax.broadcasted_iota(jnp.int32, (GW, GW), 1) // HEAD_DIM
    bdm = (rs == cs).astype(F32)
    jmat = jnp.full((GW, GROUP * c), 1.0 / HEAD_DIM, BF16)
    jmat_t = jnp.full((GROUP * c, GW), 1.0 / HEAD_DIM, BF16)
    nsq = int(round(math.log2(c))) - 1

    o_rows = []
    for ch in range(tt // c):
        r0 = ch * c
        gcum = _mm_exact_lhs(tril_c, gx[r0:r0 + c])
        eg = jnp.exp(gcum)
        glast = gcum[c - 1:c, :]
        kscale = jnp.exp(glast - gcum)
        gtot = jnp.exp(glast)
        o_groups = []
        for gr in range(N_HEADS // GROUP):
            sl = slice(gr * GW, (gr + 1) * GW)
            kq = k[r0:r0 + c, sl]
            qq = q[r0:r0 + c, sl]
            vv = v[r0:r0 + c, sl]
            bb = beta[r0:r0 + c, sl]
            gq = gcum[:, sl]
            kb = kq * bb
            xk = _stack(kq, hmasks)
            kk = _mm3(_stack(kb, hmasks), xk, _NT)
            qk = _mm3(_stack(qq, hmasks), xk, _NT)
            gs = _stack(gq, hmasks)
            diff = _mm_exact_rhs(gs, jmat) - _mm_exact_lhs(jmat_t, gs, _NT)
            dm = jnp.where(tri, jnp.exp(jnp.where(tri, diff, 0.0)), 0.0)
            lmat = jnp.where(strict, kk * dm, 0.0)
            amat = qk * dm
            pinv = eye - lmat
            msq = lmat
            for _ in range(nsq):
                msq = _mm3(msq, msq)
                pinv = pinv + _mm3(pinv, msq)
            rhs = jnp.concatenate([_stack(vv * bb, hmasks), _stack(kb * eg[:, sl], hmasks)], axis=1)
            w = _mm3(pinv, rhs)
            value = _unstack(w[:, 0:GW], c, GROUP)
            kcum = _unstack(w[:, GW:2 * GW], c, GROUP)
            s = s_scr[gr]
            u = value - _mm3(kcum, s)
            o_c = _mm3(qq * eg[:, sl], s) + _unstack(_mm3(amat, _stack(u, hmasks)), c, GROUP)
            s_scr[gr] = s * gtot[:, sl] + bdm * _mm3(kq * kscale[:, sl], u, _TN)
            o_groups.append(o_c)
        o_rows.append(jnp.concatenate(o_groups, axis=1))
    o = o_rows[0] if len(o_rows) == 1 else jnp.concatenate(o_rows, axis=0)
    on = o * lax.rsqrt(_seg_mean(o * o, bd) + RMS_EPS) * og_ref[...]
    z = z_ref[0]
    o_ref[0] = on * (z * _sigmoid(z))

    @pl.when(t == nt - 1)
    def _():
        sfin_ref[0] = s_scr[...]


def _gdn(qkva, z, bg, cbuf, s0bd, cw, av, dtb, eb, eg, bd, og, *, tt, c):
    b, t, _ = qkva.shape
    ng = N_HEADS // GROUP
    full = lambda a: pl.BlockSpec(a.shape, lambda i, j: (0,) * a.ndim)
    body = functools.partial(_gdn_body, tt=tt, c=c)
    return pl.pallas_call(
        body,
        grid=(b, t // tt),
        in_specs=[
            pl.BlockSpec((1, tt, QKV), lambda i, j: (i, j, 0)),
            pl.BlockSpec((1, tt, WIDTH), lambda i, j: (i, j, 0)),
            pl.BlockSpec((1, tt, LANES), lambda i, j: (i, j, 0)),
            pl.BlockSpec((1, CONV_WIDTH - 1, QKV), lambda i, j: (i, 0, 0)),
            pl.BlockSpec((1, ng, GW, GW), lambda i, j: (i, 0, 0, 0)),
            full(cw), full(av), full(dtb), full(eb), full(eg), full(bd), full(og),
        ],
        out_specs=(pl.BlockSpec((1, tt, WIDTH), lambda i, j: (i, j, 0)),
                   pl.BlockSpec((1, ng, GW, GW), lambda i, j: (i, 0, 0, 0))),
        out_shape=(jax.ShapeDtypeStruct((b, t, WIDTH), F32),
                   jax.ShapeDtypeStruct((b, ng, GW, GW), F32)),
        scratch_shapes=[pltpu.VMEM((ng, GW, GW), F32), pltpu.VMEM((8, QKV), F32),
                        pltpu.VMEM((tt + 8, QKV), F32)],
        compiler_params=pltpu.CompilerParams(dimension_semantics=("arbitrary", "arbitrary"),
                                             vmem_limit_bytes=VMEM_LIMIT),
        name="gdn",
    )(qkva, z, bg, cbuf, s0bd, cw, av, dtb, eb, eg, bd, og)


def _top_select(gate, col, n_cand, n_valid_f):
    g = jnp.where(col < n_cand, gate, -jnp.inf)
    sel = jnp.zeros(gate.shape, F32)
    for r in range(MOBA_TOP):
        mx = jnp.max(g, axis=-1, keepdims=True)
        idx = jnp.min(jnp.where(g == mx, col, jnp.int32(1 << 30)), axis=-1, keepdims=True)
        hit = col == idx
        sel = jnp.maximum(sel, jnp.where(hit, n_valid_f[r], 0.0))
        g = jnp.where(hit, -jnp.inf, g)
    return sel


def _moba_prompt_body(far_ref, q_ref, k_ref, v_ref, km_ref, bias_ref, o_ref):
    p = pl.program_id(1)
    i = pl.program_id(2)
    blk = MOBA_BLOCK
    q = q_ref[0] * ATTN_SCALE
    lane = lax.broadcasted_iota(jnp.int32, (1, LANES), 1)
    hmask = [lane < HEAD_DIM, lane >= HEAD_DIM]
    qm = [jnp.where(m, q, 0.0) for m in hmask]
    qb = [x.astype(BF16) for x in qm]
    km = km_ref[0]
    col = lax.broadcasted_iota(jnp.int32, (blk, LANES), 1)
    n_valid = [(i > r).astype(F32) for r in range(MOBA_TOP)]
    sels = [_top_select(_mm3(qm[hh], km, _NT), col, i, n_valid) for hh in range(2)]
    rowk = lax.broadcasted_iota(jnp.int32, (blk, blk), 0)
    colk = lax.broadcasted_iota(jnp.int32, (blk, blk), 1)

    def step(n, carry, mode):
        m0, l0, m1, l1, acc = carry
        ms, ls = [m0, m1], [l0, l1]
        start = pl.multiple_of(n * blk, blk)
        kblk = k_ref[0, pl.ds(start, blk), :].astype(BF16)
        vblk = v_ref[0, pl.ds(start, blk), :]
        alphas, pv = [], None
        for hh in range(2):
            s = _dot(qb[hh], kblk, _NT)
            if mode == "own":
                s = jnp.where(rowk >= colk, s + bias_ref[hh, 0], NEG)
            else:
                bias = bias_ref[hh, 1] if mode == "prev" else far_ref[2 * p + hh]
                selcol = jnp.sum(jnp.where(col == n, sels[hh], 0.0), axis=-1, keepdims=True)
                s = jnp.where(jnp.broadcast_to(selcol, (blk, blk)) > 0.5, s + bias, NEG)
            m_new = jnp.maximum(ms[hh], jnp.max(s, axis=-1, keepdims=True))
            alpha = jnp.exp(ms[hh] - m_new)
            pr = jnp.exp(s - m_new)
            ls[hh] = alpha * ls[hh] + jnp.sum(pr, axis=-1, keepdims=True)
            ms[hh] = m_new
            alphas.append(alpha)
            vm = jnp.where(hmask[hh], vblk, 0.0).astype(BF16)
            d = _dot(pr.astype(BF16), vm)
            pv = d if pv is None else pv + d
        acc = acc * jnp.where(hmask[0], alphas[0], alphas[1]) + pv
        return ms[0], ls[0], ms[1], ls[1], acc

    init = (jnp.full((blk, 1), NEG, F32), jnp.zeros((blk, 1), F32),
            jnp.full((blk, 1), NEG, F32), jnp.zeros((blk, 1), F32),
            jnp.zeros((blk, LANES), F32))
    carry = lax.fori_loop(0, jnp.maximum(i - 1, 0), lambda n, cr: step(n, cr, "far"), init)
    carry = lax.fori_loop(jnp.maximum(i - 1, 0), i, lambda n, cr: step(n, cr, "prev"), carry)
    m0, l0, m1, l1, acc = step(i, carry, "own")
    o_ref[0] = acc / jnp.where(hmask[0], l0, l1)


def _moba_prompt(qb, kb, vb, kmp, bias_tab, far):
    b, t, _ = qb.shape
    nq = t // MOBA_BLOCK
    npair = WIDTH // LANES
    grid_spec = pltpu.PrefetchScalarGridSpec(
        num_scalar_prefetch=1,
        grid=(b, npair, nq),
        in_specs=[
            pl.BlockSpec((1, MOBA_BLOCK, LANES), lambda bi, p, i, far: (bi, i, p)),
            pl.BlockSpec((1, t, LANES), lambda bi, p, i, far: (bi, 0, p)),
            pl.BlockSpec((1, t, LANES), lambda bi, p, i, far: (bi, 0, p)),
            pl.BlockSpec((1, LANES, LANES), lambda bi, p, i, far: (bi, 0, p)),
            pl.BlockSpec((2, 2, MOBA_BLOCK, MOBA_BLOCK), lambda bi, p, i, far: (p, 0, 0, 0)),
        ],
        out_specs=pl.BlockSpec((1, MOBA_BLOCK, LANES), lambda bi, p, i, far: (bi, i, p)),
    )
    return pl.pallas_call(
        _moba_prompt_body,
        grid_spec=grid_spec,
        out_shape=jax.ShapeDtypeStruct((b, t, WIDTH), F32),
        compiler_params=pltpu.CompilerParams(
            dimension_semantics=("arbitrary", "arbitrary", "arbitrary"),
            vmem_limit_bytes=VMEM_LIMIT),
        name="moba_prompt",
    )(far, qb, kb, vb, kmp, bias_tab)


def _head_masks(width):
    lane_h = lax.broadcasted_iota(jnp.int32, (1, width), 1) // HEAD_DIM
    return [lane_h == h for h in range(width // HEAD_DIM)]


def _dec_scores_body(pt_ref, q_ref, *refs, pps):
    k_refs = refs[:pps]
    s_ref, km_ref = refs[pps], refs[pps + 1]
    masks = _head_masks(WIDTH)
    qs = _stack(q_ref[0] * ATTN_SCALE, masks).astype(BF16)
    ppb = MOBA_BLOCK // LANES
    for i in range(pps):
        s_ref[0, :, i * LANES:(i + 1) * LANES] = _dot(qs, k_refs[i][0].astype(BF16), _NT)
    for blk in range(pps // ppb):
        tot = jnp.sum(k_refs[blk * ppb][0], axis=0, keepdims=True)
        for j in range(1, ppb):
            tot = tot + jnp.sum(k_refs[blk * ppb + j][0], axis=0, keepdims=True)
        km_ref[0, 0, blk:blk + 1, :] = tot * (1.0 / MOBA_BLOCK)


def _page_specs(pps, npages):
    def mk(i):
        return pl.BlockSpec((1, LANES, WIDTH), lambda b, j, pt: (pt[b * npages + j * pps + i], 0, 0))
    return [mk(i) for i in range(pps)]


def _dec_scores(pt_flat, q, cache, *, npages, pps):
    b, tq, _ = q.shape
    ppb = MOBA_BLOCK // LANES
    nj = npages // pps
    grid_spec = pltpu.PrefetchScalarGridSpec(
        num_scalar_prefetch=1,
        grid=(b, nj),
        in_specs=[pl.BlockSpec((1, tq, WIDTH), lambda bi, j, pt: (bi, 0, 0))] + _page_specs(pps, npages),
        out_specs=(pl.BlockSpec((1, N_HEADS * tq, pps * LANES), lambda bi, j, pt: (bi, 0, j)),
                   pl.BlockSpec((1, 1, pps // ppb, WIDTH), lambda bi, j, pt: (bi, j, 0, 0))),
    )
    return pl.pallas_call(
        functools.partial(_dec_scores_body, pps=pps),
        grid_spec=grid_spec,
        out_shape=(jax.ShapeDtypeStruct((b, N_HEADS * tq, npages * LANES), F32),
                   jax.ShapeDtypeStruct((b, nj, pps // ppb, WIDTH), F32)),
        compiler_params=pltpu.CompilerParams(dimension_semantics=("arbitrary", "arbitrary"),
                                             vmem_limit_bytes=VMEM_LIMIT),
        name="dec_scores",
    )(pt_flat, q, *([cache] * pps))


def _dec_select_body(s_ref, km_ref, q_ref, kn_ref, vn_ref, bp_ref, bo_ref, ex_ref, p_ref, oo_ref,
                     *, tq, n_past):
    rows = N_HEADS * tq
    masks = _head_masks(WIDTH)
    qs = _stack(q_ref[0] * ATTN_SCALE, masks)
    col = lax.broadcasted_iota(jnp.int32, (rows, LANES), 1)
    n_valid = [jnp.float32(1.0 if n_past > r else 0.0) for r in range(MOBA_TOP)]
    sel = _top_select(_mm3(qs, km_ref[0], _NT), col, n_past, n_valid)
    selk = _dot(sel.astype(BF16), ex_ref[...])
    logit = jnp.where(selk > 0.5, s_ref[0] + bp_ref[...], NEG)
    pad = jnp.zeros((LANES - tq, WIDTH), F32)
    kn = jnp.concatenate([kn_ref[0], pad], axis=0).astype(BF16)
    vn = jnp.concatenate([vn_ref[0], pad], axis=0).astype(BF16)
    trow = lax.rem(lax.broadcasted_iota(jnp.int32, (rows, LANES), 0), tq)
    s_own = jnp.where(col <= trow, _dot(qs.astype(BF16), kn, _NT) + bo_ref[...], NEG)
    m = jnp.maximum(jnp.max(logit, axis=-1, keepdims=True), jnp.max(s_own, axis=-1, keepdims=True))
    pr = jnp.exp(logit - m)
    po = jnp.exp(s_own - m)
    inv = 1.0 / (jnp.sum(pr, axis=-1, keepdims=True) + jnp.sum(po, axis=-1, keepdims=True))
    p_ref[0] = (pr * inv).astype(BF16)
    oo_ref[0] = _dot((po * inv).astype(BF16), vn)


def _dec_select(scores, kmp, q, kn, vn, bias_past, bias_own, expand, *, n_past):
    b, tq, _ = q.shape
    rows = N_HEADS * tq
    plen = scores.shape[2]
    full = lambda a: pl.BlockSpec(a.shape, lambda i: (0,) * a.ndim)
    per = lambda a: pl.BlockSpec((1,) + a.shape[1:], lambda i: (i,) + (0,) * (a.ndim - 1))
    return pl.pallas_call(
        functools.partial(_dec_select_body, tq=tq, n_past=n_past),
        grid=(b,),
        in_specs=[per(scores), per(kmp), per(q), per(kn), per(vn), full(bias_past), full(bias_own),
                  full(expand)],
        out_specs=(pl.BlockSpec((1, rows, plen), lambda i: (i, 0, 0)),
                   pl.BlockSpec((1, rows, WIDTH), lambda i: (i, 0, 0))),
        out_shape=(jax.ShapeDtypeStruct((b, rows, plen), BF16),
                   jax.ShapeDtypeStruct((b, rows, WIDTH), F32)),
        compiler_params=pltpu.CompilerParams(dimension_semantics=("arbitrary",),
                                             vmem_limit_bytes=VMEM_LIMIT),
        name="dec_select",
    )(scores, kmp, q, kn, vn, bias_past, bias_own, expand)


def _dec_pv_body(pt_ref, p_ref, oo_ref, *refs, pps, tq):
    v_refs = refs[:pps]
    o_ref, acc = refs[pps], refs[pps + 1]
    j = pl.program_id(1)

    @pl.when(j == 0)
    def _():
        acc[...] = oo_ref[0]

    tot = acc[...]
    for i in range(pps):
        tot = tot + _dot(p_ref[0, :, i * LANES:(i + 1) * LANES], v_refs[i][0].astype(BF16))
    acc[...] = tot

    @pl.when(j == pl.num_programs(1) - 1)
    def _():
        masks = _head_masks(WIDTH)
        a = acc[...]
        out = jnp.where(masks[0], a[0:tq], 0.0)
        for h in range(1, N_HEADS):
            out = out + jnp.where(masks[h], a[h * tq:(h + 1) * tq], 0.0)
        o_ref[0] = out


def _dec_pv(pt_flat, probs, o_own, cache, *, npages, pps, tq):
    b, rows, _ = probs.shape
    grid_spec = pltpu.PrefetchScalarGridSpec(
        num_scalar_prefetch=1,
        grid=(b, npages // pps),
        in_specs=[pl.BlockSpec((1, rows, pps * LANES), lambda bi, j, pt: (bi, 0, j)),
                  pl.BlockSpec((1, rows, WIDTH), lambda bi, j, pt: (bi, 0, 0))] + _page_specs(pps, npages),
        out_specs=pl.BlockSpec((1, tq, WIDTH), lambda bi, j, pt: (bi, 0, 0)),
        scratch_shapes=[pltpu.VMEM((rows, WIDTH), F32)],
    )
    return pl.pallas_call(
        functools.partial(_dec_pv_body, pps=pps, tq=tq),
        grid_spec=grid_spec,
        out_shape=jax.ShapeDtypeStruct((b, tq, WIDTH), F32),
        compiler_params=pltpu.CompilerParams(dimension_semantics=("arbitrary", "arbitrary"),
                                             vmem_limit_bytes=VMEM_LIMIT),
        name="dec_pv",
    )(pt_flat, probs, o_own, *([cache] * pps))


def _outproj_body(oa_ref, ob_ref, x_ref, w_ref, g_ref, wr_ref, br_ref, x1_ref, xn_ref, gate_ref, exp_ref):
    mixed = _dot(oa_ref[...].astype(BF16), w_ref[0:WIDTH, :]) + _dot(ob_ref[...].astype(BF16),
                                                                      w_ref[WIDTH:2 * WIDTH, :])
    x1 = x_ref[...] + mixed
    x1_ref[...] = x1
    ms = jnp.mean(x1 * x1, axis=-1, keepdims=True)
    xn = x1 * lax.rsqrt(ms + RMS_EPS) * g_ref[...]
    xn_ref[...] = xn.astype(BF16)
    logits = _mm3(xn, wr_ref[...]) + br_ref[...]
    col = lax.broadcasted_iota(jnp.int32, logits.shape, 1)
    g = jnp.where(col < N_EXPERTS, logits, -jnp.inf)
    vals, idxs = [], []
    for _ in range(TOP_K):
        mx = jnp.max(g, axis=-1, keepdims=True)
        idx = jnp.min(jnp.where(g == mx, col, jnp.int32(1 << 30)), axis=-1, keepdims=True)
        vals.append(mx)
        idxs.append(idx)
        g = jnp.where(col == idx, -jnp.inf, g)
    es = [jnp.exp(vv - vals[0]) for vv in vals]
    den = es[0]
    for e in es[1:]:
        den = den + e
    gates = jnp.zeros(logits.shape, F32)
    experts = jnp.zeros(logits.shape, jnp.int32)
    for kk in range(TOP_K):
        gates = jnp.where(col == kk, es[kk] / den, gates)
        experts = jnp.where(col == kk, idxs[kk], experts)
    gate_ref[...] = gates
    exp_ref[...] = experts


def _outproj(oa, ob, x2d, w_out, g2, wr, br):
    n = x2d.shape[0]
    tm = ROW_TILE
    row = lambda w: pl.BlockSpec((tm, w), lambda i: (i, 0))
    full = lambda a: pl.BlockSpec(a.shape, lambda i: (0,) * a.ndim)
    return pl.pallas_call(
        _outproj_body,
        grid=(n // tm,),
        in_specs=[row(WIDTH), row(WIDTH), row(D_MODEL), full(w_out), full(g2), full(wr), full(br)],
        out_specs=(row(D_MODEL), row(D_MODEL), row(LANES), row(LANES)),
        out_shape=(jax.ShapeDtypeStruct((n, D_MODEL), F32), jax.ShapeDtypeStruct((n, D_MODEL), BF16),
                   jax.ShapeDtypeStruct((n, LANES), F32), jax.ShapeDtypeStruct((n, LANES), jnp.int32)),
        compiler_params=pltpu.CompilerParams(dimension_semantics=("parallel",),
                                             vmem_limit_bytes=VMEM_LIMIT),
        name="outproj",
    )(oa, ob, x2d, w_out, g2, wr, br)


def _moe_body(be_ref, nv_ref, x_ref, wg_ref, bg_ref, wu_ref, bu_ref, wd_ref, bdn_ref, y_ref):
    i = pl.program_id(0)

    @pl.when(i < nv_ref[0])
    def _():
        x = x_ref[...]
        gate = jnp.minimum(_dot(x, wg_ref[0]) + bg_ref[0], SWIGLU_LIMIT)
        up = jnp.clip(_dot(x, wu_ref[0]) + bu_ref[0], -SWIGLU_LIMIT, SWIGLU_LIMIT)
        glu = gate * _sigmoid(SWIGLU_ALPHA * gate)
        hmid = ((up + 1.0) * glu).astype(BF16)
        y_ref[...] = _dot(hmid, wd_ref[0]) + bdn_ref[0]

    @pl.when(i >= nv_ref[0])
    def _():
        y_ref[...] = jnp.zeros_like(y_ref)


def _moe(block_e, nvalid, xb, wg, bg, wu, bu, wd, bdn):
    n_rows = xb.shape[0]
    nb = n_rows // MOE_ROWS
    d_ff = wg.shape[2]
    wspec = lambda s: pl.BlockSpec((1,) + s, lambda i, be, nv: (be[i], 0, 0))
    grid_spec = pltpu.PrefetchScalarGridSpec(
        num_scalar_prefetch=2,
        grid=(nb,),
        in_specs=[pl.BlockSpec((MOE_ROWS, D_MODEL), lambda i, be, nv: (i, 0)),
                  wspec((D_MODEL, d_ff)), wspec((1, d_ff)), wspec((D_MODEL, d_ff)), wspec((1, d_ff)),
                  wspec((d_ff, D_MODEL)), wspec((1, D_MODEL))],
        out_specs=pl.BlockSpec((MOE_ROWS, D_MODEL), lambda i, be, nv: (i, 0)),
    )
    return pl.pallas_call(
        _moe_body,
        grid_spec=grid_spec,
        out_shape=jax.ShapeDtypeStruct((n_rows, D_MODEL), F32),
        compiler_params=pltpu.CompilerParams(dimension_semantics=("arbitrary",),
                                             vmem_limit_bytes=VMEM_LIMIT),
        name="moe_ffn",
    )(block_e, nvalid, xb, wg, bg, wu, bu, wd, bdn)


def _bucket_np(dist):
    n = np.maximum(dist, 0)
    max_exact = NUM_BUCKETS // 2
    n_f = np.maximum(n, max_exact).astype(np.float32)
    large = max_exact + (np.log(n_f / np.float32(max_exact)) / np.float32(math.log(MAX_DISTANCE / max_exact))
                         * np.float32(NUM_BUCKETS - max_exact)).astype(np.int32)
    large = np.minimum(large, NUM_BUCKETS - 1)
    return np.where(n < max_exact, n, large).astype(np.int32)


def _blockdiag_state(s):
    b = s.shape[0]
    ng = N_HEADS // GROUP
    s5 = s.reshape(b, ng, GROUP, HEAD_DIM, HEAD_DIM)
    eye = jnp.eye(GROUP, dtype=s.dtype)
    return jnp.einsum("bghde,hk->bghdke", s5, eye).reshape(b, ng, GW, GW)


def _unblock_state(sbd):
    b = sbd.shape[0]
    ng = N_HEADS // GROUP
    s6 = sbd.reshape(b, ng, GROUP, HEAD_DIM, GROUP, HEAD_DIM)
    d = jnp.diagonal(s6, axis1=2, axis2=4)
    return jnp.moveaxis(d, -1, 2).reshape(b, N_HEADS, HEAD_DIM, HEAD_DIM)


def _mixer(x, cache_k, cache_v, page_table, s0, cbuf, rel_bias, lw):
    (g1, w_cat, bd, qg, kg, cw, av, dtb, eb, eg, og) = lw
    b, t, _ = x.shape
    x2d = x.reshape(b * t, D_MODEL)
    qkva, z, bg, qb, kb, vb, km = _inproj(x2d, g1, w_cat, bd, qg, kg)
    r3 = lambda a: a.reshape(b, t, a.shape[-1])
    qkva3 = r3(qkva)
    c = min(DELTA_CHUNK, t)
    tt = min(ROW_TILE, t)
    o_a, s_fin = _gdn(qkva3, r3(z), r3(bg), cbuf, _blockdiag_state(s0), cw, av, dtb, eb, eg, bd, og,
                      tt=tt, c=c)
    new_conv = jnp.concatenate([cbuf, qkva3], axis=1)[:, t:] if t < CONV_WIDTH - 1 else qkva3[:, t - (CONV_WIDTH - 1):]
    qb3, kb3, vb3 = r3(qb), r3(kb), r3(vb)

    if cache_k is None:
        nb = t // MOBA_BLOCK
        kmp = jnp.pad(km.reshape(b, nb, WIDTH), ((0, 0), (0, LANES - nb), (0, 0)))
        ii = np.arange(MOBA_BLOCK)
        d_own = ii[:, None] - ii[None, :]
        bidx = np.stack([_bucket_np(d_own), _bucket_np(d_own + MOBA_BLOCK)])
        far_b = _bucket_np(np.arange(MOBA_BLOCK + 1, max(t, MOBA_BLOCK + 2)))
        assert (far_b == far_b[0]).all()
        bias_tab = jnp.transpose(rel_bias.astype(F32)[bidx], (3, 0, 1, 2))
        far = rel_bias.astype(F32)[int(far_b[0])]
        o_b = _moba_prompt(qb3, kb3, vb3, kmp, bias_tab, far)
    else:
        npages = page_table.shape[1]
        page = cache_k.shape[1]
        past = npages * page
        assert page == LANES and past % MOBA_BLOCK == 0 and t <= LANES
        n_past = past // MOBA_BLOCK
        pps = 8 if npages % 8 == 0 else 2
        pt_flat = page_table.reshape(-1).astype(jnp.int32)
        scores, kmeans = _dec_scores(pt_flat, qb3, cache_k, npages=npages, pps=pps)
        kmp = jnp.pad(kmeans.reshape(b, n_past, WIDTH), ((0, 0), (0, LANES - n_past), (0, 0)))
        tpos = np.arange(t)
        d_past = past + tpos[:, None] - np.arange(past)[None, :]
        rb = rel_bias.astype(F32)
        bias_past = jnp.transpose(rb[_bucket_np(d_past)], (2, 0, 1)).reshape(N_HEADS * t, past)
        d_own = np.zeros((t, LANES), np.int64)
        d_own[:, :t] = tpos[:, None] - tpos[None, :]
        bias_own = jnp.transpose(rb[_bucket_np(d_own)], (2, 0, 1)).reshape(N_HEADS * t, LANES)
        ex = np.zeros((LANES, past), np.float32)
        ex[np.arange(past) // MOBA_BLOCK, np.arange(past)] = 1.0
        probs, o_own = _dec_select(scores, kmp, qb3, kb3, vb3, bias_past, bias_own,
                                   jnp.asarray(ex, BF16), n_past=n_past)
        o_b = _dec_pv(pt_flat, probs, o_own, cache_v, npages=npages, pps=pps, tq=t)
    return o_a.reshape(b * t, WIDTH), o_b.reshape(b * t, WIDTH), kb3, vb3, _unblock_state(s_fin), new_conv


def _moe_ffn(xn, gates, experts, x1, wg, bg, wu, bu, wd, bdn):
    n = xn.shape[0]
    nk = n * TOP_K
    flat_e = experts[:, :TOP_K].reshape(nk)
    flat_g = gates[:, :TOP_K].reshape(nk)
    onehot = (flat_e[:, None] == jnp.arange(N_EXPERTS, dtype=jnp.int32)[None, :]).astype(jnp.int32)
    csum = jnp.cumsum(onehot, axis=0)
    counts = csum[-1]
    rank = jnp.sum(csum * onehot, axis=1) - 1
    padded = ((counts + MOE_ROWS - 1) // MOE_ROWS) * MOE_ROWS
    pad_end = jnp.cumsum(padded)
    pad_start = pad_end - padded
    dest = pad_start[flat_e] + rank
    n_blocks = -(-nk // MOE_ROWS) + N_EXPERTS
    n_rows = n_blocks * MOE_ROWS
    flat_tok = jnp.arange(nk, dtype=jnp.int32) // TOP_K
    slot_tok = jnp.full((n_rows,), n, jnp.int32).at[dest].set(flat_tok, unique_indices=True)
    nvalid = (pad_end[-1] // MOE_ROWS).astype(jnp.int32).reshape(1)
    blk_start = jnp.arange(n_blocks, dtype=jnp.int32) * MOE_ROWS
    block_e = jnp.minimum(jnp.searchsorted(pad_end, blk_start, side="right"), N_EXPERTS - 1).astype(jnp.int32)
    last_e = block_e[jnp.maximum(nvalid[0] - 1, 0)]
    block_e = jnp.where(jnp.arange(n_blocks) < nvalid[0], block_e, last_e)
    x_ext = jnp.concatenate([xn, jnp.zeros((1, D_MODEL), xn.dtype)], axis=0)
    xb = x_ext[slot_tok]
    yb = _moe(block_e, nvalid, xb, wg, bg, wu, bu, wd, bdn)
    picked = yb[dest.reshape(n, TOP_K)]
    return x1 + jnp.sum(picked * flat_g.reshape(n, TOP_K, 1), axis=1)


def kernel(x_prompt, x_sample, cache_k, cache_v, state_delta, state_conv, page_table, rel_bias,
           norm1_g, w_in, conv_w, A_log, dt_bias, o_norm_g, q_norm_g, k_norm_g, w_out, norm2_g,
           w_router, b_router, w_gate, b_gate, w_up, b_up, w_down, b_down):
    depth = norm1_g.shape[0]
    bp, tp, _ = x_prompt.shape
    bs, ts, _ = x_sample.shape
    yp, ys = x_prompt, x_sample
    outs = [[] for _ in range(8)]
    seg = np.arange(WIDTH) // HEAD_DIM
    bd = jnp.asarray((seg[:, None] == seg[None, :]).astype(np.float32) / HEAD_DIM, BF16)
    eb_np = np.zeros((LANES, WIDTH), np.float32)
    eb_np[seg, np.arange(WIDTH)] = 1.0
    eg_np = np.zeros((LANES, WIDTH), np.float32)
    eg_np[N_HEADS + seg, np.arange(WIDTH)] = 1.0
    eb, eg = jnp.asarray(eb_np, BF16), jnp.asarray(eg_np, BF16)
    c0, c1, c3 = QKV, QKV + WIDTH, QKV + WIDTH + 2 * N_HEADS
    for l in range(depth):
        wl = w_in[l]
        w_cat = jnp.concatenate([wl[:, :c1], wl[:, c3:], wl[:, c1:c3],
                                 jnp.zeros((D_MODEL, LANES - 2 * N_HEADS), wl.dtype)], axis=1).astype(BF16)
        tile8 = lambda g: jnp.tile(g.astype(F32), N_HEADS).reshape(1, WIDTH)
        av = jnp.zeros((1, LANES), F32).at[0, N_HEADS:2 * N_HEADS].set(-jnp.exp(A_log[l].astype(F32)))
        dtb = jnp.zeros((1, LANES), F32).at[0, N_HEADS:2 * N_HEADS].set(dt_bias[l].astype(F32))
        lw = (norm1_g[l].astype(F32).reshape(1, D_MODEL), w_cat, bd, tile8(q_norm_g[l]), tile8(k_norm_g[l]),
              conv_w[l].astype(F32), av, dtb, eb, eg, tile8(o_norm_g[l]))
        s0_p = jnp.zeros((bp, N_HEADS, HEAD_DIM, HEAD_DIM), F32)
        c0_p = jnp.zeros((bp, CONV_WIDTH - 1, QKV), F32)
        oa_p, ob_p, k_p, v_p, s_p, c_p = _mixer(yp, None, None, None, s0_p, c0_p, rel_bias, lw)
        ck = cache_k[l].reshape(cache_k.shape[1], cache_k.shape[2], WIDTH)
        cv = cache_v[l].reshape(cache_v.shape[1], cache_v.shape[2], WIDTH)
        oa_s, ob_s, k_s, v_s, s_s, c_s = _mixer(ys, ck, cv, page_table, state_delta[l], state_conv[l],
                                                rel_bias, lw)
        w_o = w_out[l].astype(BF16)
        g2 = norm2_g[l].astype(F32).reshape(1, D_MODEL)
        wr = jnp.pad(w_router[l].astype(F32), ((0, 0), (0, LANES - N_EXPERTS)))
        br = jnp.pad(b_router[l].astype(F32), (0, LANES - N_EXPERTS)).reshape(1, LANES)
        x1_p, xn_p, gt_p, ex_p = _outproj(oa_p, ob_p, yp.reshape(bp * tp, D_MODEL), w_o, g2, wr, br)
        x1_s, xn_s, gt_s, ex_s = _outproj(oa_s, ob_s, ys.reshape(bs * ts, D_MODEL), w_o, g2, wr, br)
        cat = lambda a, b_: jnp.concatenate([a, b_], axis=0)
        y_all = _moe_ffn(cat(xn_p, xn_s), cat(gt_p, gt_s), cat(ex_p, ex_s), cat(x1_p, x1_s),
                         w_gate[l].astype(BF16), b_gate[l].astype(F32)[:, None, :],
                         w_up[l].astype(BF16), b_up[l].astype(F32)[:, None, :],
                         w_down[l].astype(BF16), b_down[l].astype(F32)[:, None, :])
        yp = y_all[:bp * tp].reshape(bp, tp, D_MODEL)
        ys = y_all[bp * tp:].reshape(bs, ts, D_MODEL)
        shp = lambda a, b_, t_: a.reshape(b_, t_, N_HEADS, HEAD_DIM)
        for lst, val in zip(outs, (shp(k_p, bp, tp), shp(v_p, bp, tp), s_p, c_p,
                                   shp(k_s, bs, ts), shp(v_s, bs, ts), s_s, c_s)):
            lst.append(val)
    stacked = [jnp.stack(o) for o in outs]
    return (yp, ys, *stacked)
```

```python
import functools
import math

import numpy as np
import jax
import jax.numpy as jnp
from jax import lax
from jax.experimental import pallas as pl
from jax.experimental.pallas import tpu as pltpu

F32 = jnp.float32
BF16 = jnp.bfloat16

D_MODEL = 1024
N_HEADS = 8
HEAD_DIM = 64
WIDTH = N_HEADS * HEAD_DIM
QKV = 3 * WIDTH
CONV_WIDTH = 4
DELTA_CHUNK = 64
MOBA_BLOCK = 256
MOBA_TOP = 3
ATTN_SCALE = HEAD_DIM ** -0.5
NUM_BUCKETS = 32
MAX_DISTANCE = 128
N_EXPERTS = 32
TOP_K = 4
SWIGLU_LIMIT = 7.0
SWIGLU_ALPHA = 1.702
RMS_EPS = 1e-6

LANES = 128
SUBLANES = 8
ROW_TILE = 256
MOE_ROWS = 256
NEG = -1e30
VMEM_LIMIT = 48 * 1024 * 1024

GROUP = 4
GW = GROUP * HEAD_DIM
NPAIR = WIDTH // LANES
FAR_BLOCKS = 4

_NN = (((1,), (0,)), ((), ()))
_NT = (((1,), (1,)), ((), ()))
_TN = (((0,), (0,)), ((), ()))


def _dot(a, b, dims=_NN):
    return lax.dot_general(a, b, dims, preferred_element_type=F32)


def _split2(a):
    hi = a.astype(BF16)
    lo = (a - hi.astype(F32)).astype(BF16)
    return hi, lo


def _split3(a):
    hi = a.astype(BF16)
    r = a - hi.astype(F32)
    mid = r.astype(BF16)
    lo = (r - mid.astype(F32)).astype(BF16)
    return hi, mid, lo


def _mm1(a, b, dims=_NN):
    return _dot(a.astype(BF16), b.astype(BF16), dims)


def _mm3(a, b, dims=_NN):
    ah, al = _split2(a)
    bh, bl = _split2(b)
    return _dot(ah, bh, dims) + (_dot(ah, bl, dims) + _dot(al, bh, dims))


def _mm_exact_rhs(a, b_bf16, dims=_NN):
    hi, mid, lo = _split3(a)
    return _dot(hi, b_bf16, dims) + (_dot(mid, b_bf16, dims) + _dot(lo, b_bf16, dims))


def _mm_exact_lhs(a_bf16, b, dims=_NN):
    hi, mid, lo = _split3(b)
    return _dot(a_bf16, hi, dims) + (_dot(a_bf16, mid, dims) + _dot(a_bf16, lo, dims))


def _seg_mean(x, bd):
    outs = []
    for g in range(x.shape[1] // GW):
        hi, lo = _split2(x[:, g * GW:(g + 1) * GW])
        outs.append(_dot(hi, bd) + _dot(lo, bd))
    return outs[0] if len(outs) == 1 else jnp.concatenate(outs, axis=1)


def _sigmoid(x):
    return 1.0 / (1.0 + jnp.exp(-x))


def _softplus(x):
    return jnp.maximum(x, 0.0) + jnp.log(1.0 + jnp.exp(-jnp.abs(x)))


def _inproj_body(x_ref, g_ref, w_ref, bd_ref, qg_ref, kg_ref,
                 qkva_ref, z_ref, bg_ref, qb_ref, kb_ref, vb_ref, *attn_refs):
    x = x_ref[...]
    ms = jnp.mean(x * x, axis=-1, keepdims=True)
    h = (x * lax.rsqrt(ms + RMS_EPS) * g_ref[...]).astype(BF16)

    def proj(lo, hi):
        return _dot(h, w_ref[:, lo:hi])

    bd = bd_ref[...]
    qkva_ref[...] = proj(0, QKV)
    z_ref[...] = proj(QKV, QKV + WIDTH)
    c = QKV + WIDTH
    q = proj(c, c + WIDTH)
    qb_ref[...] = q * lax.rsqrt(_seg_mean(q * q, bd) + RMS_EPS) * qg_ref[...]
    k = proj(c + WIDTH, c + 2 * WIDTH)
    kn = k * lax.rsqrt(_seg_mean(k * k, bd) + RMS_EPS) * kg_ref[...]
    kb_ref[...] = kn
    v = proj(c + 2 * WIDTH, c + 3 * WIDTH)
    vb_ref[...] = v
    bg_ref[...] = proj(c + 3 * WIDTH, c + 3 * WIDTH + LANES)
    if attn_refs:
        kbf_ref, vt_ref, km_ref = attn_refs
        kbf_ref[...] = kn.astype(BF16)
        km_ref[0] = jnp.mean(kn, axis=0, keepdims=True)
        vt = v.T.astype(BF16)
        for pp in range(NPAIR):
            vt_ref[0, pp, 0] = vt[pp * LANES:(pp + 1) * LANES, :]


def _inproj(x2d, g1, w_cat, bd, qg, kg, *, seq_len, attn_layout):
    n = x2d.shape[0]
    tm = ROW_TILE
    nt = n // tm
    row = lambda w: pl.BlockSpec((tm, w), lambda i: (i, 0))
    full = lambda a: pl.BlockSpec(a.shape, lambda i: (0,) * a.ndim)
    out_shape = [
        jax.ShapeDtypeStruct((n, QKV), F32), jax.ShapeDtypeStruct((n, WIDTH), F32),
        jax.ShapeDtypeStruct((n, LANES), F32), jax.ShapeDtypeStruct((n, WIDTH), F32),
        jax.ShapeDtypeStruct((n, WIDTH), F32), jax.ShapeDtypeStruct((n, WIDTH), F32),
    ]
    out_specs = [row(QKV), row(WIDTH), row(LANES), row(WIDTH), row(WIDTH), row(WIDTH)]
    if attn_layout:
        tpb = seq_len // tm
        out_shape += [jax.ShapeDtypeStruct((n, WIDTH), BF16),
                      jax.ShapeDtypeStruct((n // seq_len, NPAIR, tpb, LANES, tm), BF16),
                      jax.ShapeDtypeStruct((nt, 1, WIDTH), F32)]
        out_specs += [row(WIDTH),
                      pl.BlockSpec((1, NPAIR, 1, LANES, tm), lambda i: (i // tpb, 0, i % tpb, 0, 0)),
                      pl.BlockSpec((1, 1, WIDTH), lambda i: (i, 0, 0))]
    return pl.pallas_call(
        _inproj_body,
        grid=(nt,),
        in_specs=[row(D_MODEL), full(g1), full(w_cat), full(bd), full(qg), full(kg)],
        out_specs=tuple(out_specs),
        out_shape=tuple(out_shape),
        compiler_params=pltpu.CompilerParams(dimension_semantics=("parallel",),
                                             vmem_limit_bytes=VMEM_LIMIT),
        name="inproj",
    )(x2d, g1, w_cat, bd, qg, kg)


def _stack(x, masks):
    return jnp.concatenate([jnp.where(m, x, 0.0) for m in masks], axis=0)


def _unstack(w, c, n):
    acc = w[0:c]
    for h in range(1, n):
        acc = acc + w[h * c:(h + 1) * c]
    return acc


def _gdn_body(qkva_ref, z_ref, bg_ref, cbuf_ref, s0_ref, cw_ref, av_ref, dtb_ref, eb_ref, eg_ref,
              bd_ref, og_ref, o_ref, sfin_ref, s_scr, carry_scr, uext_scr, *, tt, c):
    t = pl.program_id(1)
    nt = pl.num_programs(1)
    gc = GROUP * c

    @pl.when(t == 0)
    def _():
        s_scr[...] = s0_ref[0]
        carry_scr[...] = jnp.zeros_like(carry_scr)
        carry_scr[SUBLANES - (CONV_WIDTH - 1):SUBLANES, :] = cbuf_ref[0]

    uext_scr[0:SUBLANES, :] = carry_scr[...]
    uext_scr[SUBLANES:SUBLANES + tt, :] = qkva_ref[0]
    carry_scr[...] = uext_scr[tt:tt + SUBLANES, :]

    cw = cw_ref[...]
    off = SUBLANES - (CONV_WIDTH - 1)
    conv = cw[0:1] * uext_scr[off:off + tt, :]
    for i in range(1, CONV_WIDTH):
        conv = conv + cw[i:i + 1] * uext_scr[off + i:off + i + tt, :]
    act = conv * _sigmoid(conv)

    bd = bd_ref[...]
    q = act[:, 0:WIDTH]
    k = act[:, WIDTH:2 * WIDTH]
    v = act[:, 2 * WIDTH:3 * WIDTH]
    q = q * lax.rsqrt(_seg_mean(q * q, bd) * HEAD_DIM + RMS_EPS) * (HEAD_DIM ** -0.5)
    k = k * lax.rsqrt(_seg_mean(k * k, bd) * HEAD_DIM + RMS_EPS)

    bg = bg_ref[0]
    beta = _mm_exact_rhs(_sigmoid(bg), eb_ref[...])
    gx = _mm_exact_rhs(av_ref[...] * _softplus(bg + dtb_ref[...]), eg_ref[...])

    r4 = lax.broadcasted_iota(jnp.int32, (gc, gc), 0)
    c4 = lax.broadcasted_iota(jnp.int32, (gc, gc), 1)
    same = (r4 // c) == (c4 // c)
    tri = jnp.logical_and(same, r4 >= c4)
    strict = jnp.logical_and(same, r4 > c4)
    eye = (r4 == c4).astype(F32)
    rc = lax.broadcasted_iota(jnp.int32, (c, c), 0)
    cc = lax.broadcasted_iota(jnp.int32, (c, c), 1)
    tril_c = (rc >= cc).astype(BF16)
    lane_g = lax.broadcasted_iota(jnp.int32, (1, GW), 1) // HEAD_DIM
    hmasks = [lane_g == h for h in range(GROUP)]
    rs = lax.broadcasted_iota(jnp.int32, (GW, GW), 0) // HEAD_DIM
    cs = lax.broadcasted_iota(jnp.int32, (GW, GW), 1) // HEAD_DIM
    bdm = (rs == cs).astype(F32)
    nsq = int(round(math.log2(c))) - 1

    o_rows = []
    for ch in range(tt // c):
        r0 = ch * c
        gcum = _mm_exact_lhs(tril_c, gx[r0:r0 + c])
        eg = jnp.exp(gcum)
        glast = gcum[c - 1:c, :]
        kscale = jnp.exp(glast - gcum)
        gtot = jnp.exp(glast)
        o_groups = []
        for gr in range(N_HEADS // GROUP):
            sl = slice(gr * GW, (gr + 1) * GW)
            kq = k[r0:r0 + c, sl]
            qq = q[r0:r0 + c, sl]
            vv = v[r0:r0 + c, sl]
            bb = beta[r0:r0 + c, sl]
            gq = gcum[:, sl]
            kb = kq * bb
            xk = _stack(kq, hmasks).astype(BF16)
            kk = _dot(_stack(kb, hmasks).astype(BF16), xk, _NT)
            qk = _dot(_stack(qq, hmasks).astype(BF16), xk, _NT)
            gcol = jnp.concatenate(
                [jnp.broadcast_to(gq[:, h * HEAD_DIM:h * HEAD_DIM + 1], (c, gc)) for h in range(GROUP)], axis=0)
            if gc % LANES == 0:
                grow = gcol.T
            else:
                grow = _mm_exact_lhs(jnp.full((gc, GW), 1.0 / HEAD_DIM, BF16), _stack(gq, hmasks), _NT)
            dm = jnp.where(tri, jnp.exp(jnp.where(tri, gcol - grow, 0.0)), 0.0)
            lmat = jnp.where(strict, kk * dm, 0.0)
            amat = qk * dm
            pinv = eye - lmat
            msq = lmat
            for _ in range(nsq):
                msq = _mm1(msq, msq)
                pinv = pinv + _mm1(pinv, msq)
            rhs = jnp.concatenate([_stack(vv * bb, hmasks), _stack(kb * eg[:, sl], hmasks)], axis=1)
            w = _mm1(pinv, rhs)
            value = _unstack(w[:, 0:GW], c, GROUP)
            kcum = _unstack(w[:, GW:2 * GW], c, GROUP)
            s = s_scr[gr]
            u = value - _mm3(kcum, s)
            o_c = _mm1(qq * eg[:, sl], s) + _unstack(_mm1(amat, _stack(u, hmasks)), c, GROUP)
            s_scr[gr] = s * gtot[:, sl] + bdm * _mm3(kq * kscale[:, sl], u, _TN)
            o_groups.append(o_c)
        o_rows.append(jnp.concatenate(o_groups, axis=1))
    o = o_rows[0] if len(o_rows) == 1 else jnp.concatenate(o_rows, axis=0)
    on = o * lax.rsqrt(_seg_mean(o * o, bd) + RMS_EPS) * og_ref[...]
    z = z_ref[0]
    o_ref[0] = on * (z * _sigmoid(z))

    @pl.when(t == nt - 1)
    def _():
        sfin_ref[0] = s_scr[...]


def _gdn(qkva, z, bg, cbuf, s0bd, cw, av, dtb, eb, eg, bd, og, *, tt, c):
    b, t, _ = qkva.shape
    ng = N_HEADS // GROUP
    full = lambda a: pl.BlockSpec(a.shape, lambda i, j: (0,) * a.ndim)
    body = functools.partial(_gdn_body, tt=tt, c=c)
    return pl.pallas_call(
        body,
        grid=(b, t // tt),
        in_specs=[
            pl.BlockSpec((1, tt, QKV), lambda i, j: (i, j, 0)),
            pl.BlockSpec((1, tt, WIDTH), lambda i, j: (i, j, 0)),
            pl.BlockSpec((1, tt, LANES), lambda i, j: (i, j, 0)),
            pl.BlockSpec((1, CONV_WIDTH - 1, QKV), lambda i, j: (i, 0, 0)),
            pl.BlockSpec((1, ng, GW, GW), lambda i, j: (i, 0, 0, 0)),
            full(cw), full(av), full(dtb), full(eb), full(eg), full(bd), full(og),
        ],
        out_specs=(pl.BlockSpec((1, tt, WIDTH), lambda i, j: (i, j, 0)),
                   pl.BlockSpec((1, ng, GW, GW), lambda i, j: (i, 0, 0, 0))),
        out_shape=(jax.ShapeDtypeStruct((b, t, WIDTH), F32),
                   jax.ShapeDtypeStruct((b, ng, GW, GW), F32)),
        scratch_shapes=[pltpu.VMEM((ng, GW, GW), F32), pltpu.VMEM((SUBLANES, QKV), F32),
                        pltpu.VMEM((tt + SUBLANES, QKV), F32)],
        compiler_params=pltpu.CompilerParams(dimension_semantics=("arbitrary", "arbitrary"),
                                             vmem_limit_bytes=VMEM_LIMIT),
        name="gdn",
    )(qkva, z, bg, cbuf, s0bd, cw, av, dtb, eb, eg, bd, og)


def _top_select(gate, idx, n_cand, n_valid_f, axis):
    g = jnp.where(idx < n_cand, gate, -jnp.inf)
    sel = jnp.zeros(gate.shape, F32)
    for r in range(MOBA_TOP):
        mx = jnp.max(g, axis=axis, keepdims=True)
        first = jnp.min(jnp.where(g == mx, idx, jnp.int32(1 << 30)), axis=axis, keepdims=True)
        hit = idx == first
        sel = jnp.maximum(sel, jnp.where(hit, n_valid_f[r], 0.0))
        g = jnp.where(hit, -jnp.inf, g)
    return sel


def _moba_prompt_body(far_ref, q_ref, k_ref, vt_ref, km_ref, bias_ref, o_ref, sel_scr):
    p = pl.program_id(1)
    i = pl.program_id(2)
    blk = MOBA_BLOCK
    qt = (q_ref[0] * ATTN_SCALE).T
    row = lax.broadcasted_iota(jnp.int32, (LANES, blk), 0)
    top = row < HEAD_DIM
    qt_m = [jnp.where(top, qt, 0.0), jnp.where(top, 0.0, qt)]
    qtb = [x.astype(BF16) for x in qt_m]
    km = km_ref[0]
    n_valid = [(i > r).astype(F32) for r in range(MOBA_TOP)]
    for hh in range(2):
        sel_scr[hh] = _top_select(_mm3(km, qt_m[hh]), row, i, n_valid, 0)
    rowk = lax.broadcasted_iota(jnp.int32, (blk, blk), 0)
    colq = lax.broadcasted_iota(jnp.int32, (blk, blk), 1)

    def step(n, carry, modes):
        m0, l0, m1, l1, acc = carry
        ms, ls = [m0, m1], [l0, l1]
        nblk = len(modes)
        kblks = [k_ref[0, pl.ds(pl.multiple_of((n + j) * blk, blk), blk), :] for j in range(nblk)]
        vts = [vt_ref[0, 0, n + j] for j in range(nblk)]
        alphas, pvs = [], []
        for hh in range(2):
            ss, shifts, valids, bmax = [], [], [], []
            for j, mode in enumerate(modes):
                s = _dot(kblks[j], qtb[hh])
                if mode == "own":
                    s = jnp.where(rowk <= colq, s + bias_ref[hh, 0], NEG)
                    shift, valid = 0.0, None
                else:
                    if mode == "prev":
                        s, shift = s + bias_ref[hh, 1], 0.0
                    else:
                        shift = far_ref[2 * p + hh]
                    valid = sel_scr[hh, pl.ds(n + j, 1), :] > 0.5
                bm = jnp.max(s, axis=0, keepdims=True) + shift
                ss.append(s)
                shifts.append(shift)
                valids.append(valid)
                bmax.append(bm if valid is None else jnp.where(valid, bm, NEG))
            m_new = ms[hh]
            for bm in bmax:
                m_new = jnp.maximum(m_new, bm)
            alpha = jnp.exp(ms[hh] - m_new)
            lsum = alpha * ls[hh]
            pv = None
            for j, s in enumerate(ss):
                off = m_new - shifts[j]
                if valids[j] is not None:
                    off = jnp.where(valids[j], off, -NEG)
                pr = jnp.exp(s - off)
                lsum = lsum + jnp.sum(pr, axis=0, keepdims=True)
                d = _dot(vts[j], pr.astype(BF16))
                pv = d if pv is None else pv + d
            ls[hh] = lsum
            ms[hh] = m_new
            alphas.append(alpha)
            pvs.append(pv)
        acc = acc * jnp.where(top, alphas[0], alphas[1]) + jnp.where(top, pvs[0], pvs[1])
        return ms[0], ls[0], ms[1], ls[1], acc

    init = (jnp.full((1, blk), NEG, F32), jnp.zeros((1, blk), F32),
            jnp.full((1, blk), NEG, F32), jnp.zeros((1, blk), F32),
            jnp.zeros((LANES, blk), F32))
    n_far = jnp.maximum(i - 1, 0)
    n_wide = n_far // FAR_BLOCKS
    carry = lax.fori_loop(0, n_wide, lambda w, cr: step(w * FAR_BLOCKS, cr, ("far",) * FAR_BLOCKS), init)
    done = n_wide * FAR_BLOCKS
    width = FAR_BLOCKS // 2
    while width >= 1:
        take = ((n_far - done) >= width).astype(jnp.int32)
        carry = lax.fori_loop(0, take, lambda _, cr, d=done, w=width: step(d, cr, ("far",) * w), carry)
        done = done + take * width
        width //= 2
    has_prev = (i >= 1).astype(jnp.int32)
    carry = lax.fori_loop(0, has_prev, lambda _, cr: step(i - 1, cr, ("prev", "own")), carry)
    m0, l0, m1, l1, acc = lax.fori_loop(0, 1 - has_prev, lambda _, cr: step(i, cr, ("own",)), carry)
    o_ref[0] = (acc / jnp.where(top, l0, l1)).T


def _moba_prompt(qb, kbf, vt, kmp, bias_tab, far):
    b, t, _ = qb.shape
    nq = t // MOBA_BLOCK
    grid_spec = pltpu.PrefetchScalarGridSpec(
        num_scalar_prefetch=1,
        grid=(b, NPAIR, nq),
        in_specs=[
            pl.BlockSpec((1, MOBA_BLOCK, LANES), lambda bi, p, i, far: (bi, i, p)),
            pl.BlockSpec((1, t, LANES), lambda bi, p, i, far: (bi, 0, p)),
            pl.BlockSpec((1, 1, nq, LANES, MOBA_BLOCK), lambda bi, p, i, far: (bi, p, 0, 0, 0)),
            pl.BlockSpec((1, LANES, LANES), lambda bi, p, i, far: (bi, 0, p)),
            pl.BlockSpec((2, 2, MOBA_BLOCK, MOBA_BLOCK), lambda bi, p, i, far: (p, 0, 0, 0)),
        ],
        out_specs=pl.BlockSpec((1, MOBA_BLOCK, LANES), lambda bi, p, i, far: (bi, i, p)),
        scratch_shapes=[pltpu.VMEM((2, LANES, MOBA_BLOCK), F32)],
    )
    return pl.pallas_call(
        _moba_prompt_body,
        grid_spec=grid_spec,
        out_shape=jax.ShapeDtypeStruct((b, t, WIDTH), F32),
        compiler_params=pltpu.CompilerParams(
            dimension_semantics=("arbitrary", "arbitrary", "arbitrary"),
            vmem_limit_bytes=VMEM_LIMIT),
        name="moba_prompt",
    )(far, qb, kbf, vt, kmp, bias_tab)


def _head_masks(width):
    lane_h = lax.broadcasted_iota(jnp.int32, (1, width), 1) // HEAD_DIM
    return [lane_h == h for h in range(width // HEAD_DIM)]


def _dec_scores_body(pt_ref, q_ref, *refs, pps):
    k_refs = refs[:pps]
    s_ref, km_ref = refs[pps], refs[pps + 1]
    masks = _head_masks(WIDTH)
    qs = _stack(q_ref[0] * ATTN_SCALE, masks).astype(BF16)
    ones = jnp.ones((SUBLANES, LANES), BF16)
    ppb = MOBA_BLOCK // LANES
    sums = []
    for i in range(pps):
        hi, lo = _split2(k_refs[i][0])
        s_ref[0, :, i * LANES:(i + 1) * LANES] = _dot(qs, hi)
        sums.append(_dot(ones, hi, _NT) + _dot(ones, lo, _NT))
    for blk in range(pps // ppb):
        tot = sums[blk * ppb]
        for j in range(1, ppb):
            tot = tot + sums[blk * ppb + j]
        km_ref[0, 0, blk:blk + 1, :] = tot[0:1] * (1.0 / MOBA_BLOCK)


def _page_specs(pps, npages):
    def mk(i):
        return pl.BlockSpec((1, WIDTH, LANES), lambda b, j, pt: (pt[b * npages + j * pps + i], 0, 0))
    return [mk(i) for i in range(pps)]


def _dec_scores(pt_flat, q, cache, *, npages, pps):
    b, tq, _ = q.shape
    ppb = MOBA_BLOCK // LANES
    nj = npages // pps
    grid_spec = pltpu.PrefetchScalarGridSpec(
        num_scalar_prefetch=1,
        grid=(b, nj),
        in_specs=[pl.BlockSpec((1, tq, WIDTH), lambda bi, j, pt: (bi, 0, 0))] + _page_specs(pps, npages),
        out_specs=(pl.BlockSpec((1, N_HEADS * tq, pps * LANES), lambda bi, j, pt: (bi, 0, j)),
                   pl.BlockSpec((1, 1, pps // ppb, WIDTH), lambda bi, j, pt: (bi, j, 0, 0))),
    )
    return pl.pallas_call(
        functools.partial(_dec_scores_body, pps=pps),
        grid_spec=grid_spec,
        out_shape=(jax.ShapeDtypeStruct((b, N_HEADS * tq, npages * LANES), F32),
                   jax.ShapeDtypeStruct((b, nj, pps // ppb, WIDTH), F32)),
        compiler_params=pltpu.CompilerParams(dimension_semantics=("arbitrary", "arbitrary"),
                                             vmem_limit_bytes=VMEM_LIMIT),
        name="dec_scores",
    )(pt_flat, q, *([cache] * pps))


def _dec_select_body(s_ref, km_ref, q_ref, kn_ref, vn_ref, bp_ref, bo_ref, ex_ref, p_ref, oo_ref,
                     *, tq, n_past):
    rows = N_HEADS * tq
    masks = _head_masks(WIDTH)
    qs = _stack(q_ref[0] * ATTN_SCALE, masks)
    col = lax.broadcasted_iota(jnp.int32, (rows, LANES), 1)
    n_valid = [jnp.float32(1.0 if n_past > r else 0.0) for r in range(MOBA_TOP)]
    sel = _top_select(_mm3(qs, km_ref[0], _NT), col, n_past, n_valid, -1)
    selk = _dot(sel.astype(BF16), ex_ref[...])
    logit = jnp.where(selk > 0.5, s_ref[0] + bp_ref[...], NEG)
    pad = jnp.zeros((LANES - tq, WIDTH), F32)
    kn = jnp.concatenate([kn_ref[0], pad], axis=0).astype(BF16)
    vn = jnp.concatenate([vn_ref[0], pad], axis=0).astype(BF16)
    trow = lax.rem(lax.broadcasted_iota(jnp.int32, (rows, LANES), 0), tq)
    s_own = jnp.where(col <= trow, _dot(qs.astype(BF16), kn, _NT) + bo_ref[...], NEG)
    m = jnp.maximum(jnp.max(logit, axis=-1, keepdims=True), jnp.max(s_own, axis=-1, keepdims=True))
    pr = jnp.exp(logit - m)
    po = jnp.exp(s_own - m)
    inv = 1.0 / (jnp.sum(pr, axis=-1, keepdims=True) + jnp.sum(po, axis=-1, keepdims=True))
    p_ref[0] = (pr * inv).astype(BF16)
    oo_ref[0] = _dot((po * inv).astype(BF16), vn)


def _dec_select(scores, kmp, q, kn, vn, bias_past, bias_own, expand, *, n_past):
    b, tq, _ = q.shape
    rows = N_HEADS * tq
    plen = scores.shape[2]
    full = lambda a: pl.BlockSpec(a.shape, lambda i: (0,) * a.ndim)
    per = lambda a: pl.BlockSpec((1,) + a.shape[1:], lambda i: (i,) + (0,) * (a.ndim - 1))
    return pl.pallas_call(
        functools.partial(_dec_select_body, tq=tq, n_past=n_past),
        grid=(b,),
        in_specs=[per(scores), per(kmp), per(q), per(kn), per(vn), full(bias_past), full(bias_own),
                  full(expand)],
        out_specs=(pl.BlockSpec((1, rows, plen), lambda i: (i, 0, 0)),
                   pl.BlockSpec((1, rows, WIDTH), lambda i: (i, 0, 0))),
        out_shape=(jax.ShapeDtypeStruct((b, rows, plen), BF16),
                   jax.ShapeDtypeStruct((b, rows, WIDTH), F32)),
        compiler_params=pltpu.CompilerParams(dimension_semantics=("arbitrary",),
                                             vmem_limit_bytes=VMEM_LIMIT),
        name="dec_select",
    )(scores, kmp, q, kn, vn, bias_past, bias_own, expand)


def _dec_pv_body(pt_ref, p_ref, oo_ref, *refs, pps, tq):
    v_refs = refs[:pps]
    o_ref, acc = refs[pps], refs[pps + 1]
    j = pl.program_id(1)

    @pl.when(j == 0)
    def _():
        acc[...] = oo_ref[0]

    tot = acc[...]
    for i in range(pps):
        tot = tot + _dot(p_ref[0, :, i * LANES:(i + 1) * LANES], v_refs[i][0].astype(BF16), _NT)
    acc[...] = tot

    @pl.when(j == pl.num_programs(1) - 1)
    def _():
        masks = _head_masks(WIDTH)
        a = acc[...]
        out = jnp.where(masks[0], a[0:tq], 0.0)
        for h in range(1, N_HEADS):
            out = out + jnp.where(masks[h], a[h * tq:(h + 1) * tq], 0.0)
        o_ref[0] = out


def _dec_pv(pt_flat, probs, o_own, cache, *, npages, pps, tq):
    b, rows, _ = probs.shape
    grid_spec = pltpu.PrefetchScalarGridSpec(
        num_scalar_prefetch=1,
        grid=(b, npages // pps),
        in_specs=[pl.BlockSpec((1, rows, pps * LANES), lambda bi, j, pt: (bi, 0, j)),
                  pl.BlockSpec((1, rows, WIDTH), lambda bi, j, pt: (bi, 0, 0))] + _page_specs(pps, npages),
        out_specs=pl.BlockSpec((1, tq, WIDTH), lambda bi, j, pt: (bi, 0, 0)),
        scratch_shapes=[pltpu.VMEM((rows, WIDTH), F32)],
    )
    return pl.pallas_call(
        functools.partial(_dec_pv_body, pps=pps, tq=tq),
        grid_spec=grid_spec,
        out_shape=jax.ShapeDtypeStruct((b, tq, WIDTH), F32),
        compiler_params=pltpu.CompilerParams(dimension_semantics=("arbitrary", "arbitrary"),
                                             vmem_limit_bytes=VMEM_LIMIT),
        name="dec_pv",
    )(pt_flat, probs, o_own, *([cache] * pps))


def _outproj_body(oa_ref, ob_ref, x_ref, w_ref, g_ref, wr_ref, br_ref, x1_ref, xn_ref, gate_ref, exp_ref):
    mixed = _dot(oa_ref[...].astype(BF16), w_ref[0:WIDTH, :]) + _dot(ob_ref[...].astype(BF16),
                                                                      w_ref[WIDTH:2 * WIDTH, :])
    x1 = x_ref[...] + mixed
    x1_ref[...] = x1
    ms = jnp.mean(x1 * x1, axis=-1, keepdims=True)
    xn = x1 * lax.rsqrt(ms + RMS_EPS) * g_ref[...]
    xn_ref[...] = xn.astype(BF16)
    logits = _mm3(xn, wr_ref[...]) + br_ref[...]
    col = lax.broadcasted_iota(jnp.int32, logits.shape, 1)
    g = jnp.where(col < N_EXPERTS, logits, -jnp.inf)
    vals, idxs = [], []
    for _ in range(TOP_K):
        mx = jnp.max(g, axis=-1, keepdims=True)
        idx = jnp.min(jnp.where(g == mx, col, jnp.int32(1 << 30)), axis=-1, keepdims=True)
        vals.append(mx)
        idxs.append(idx)
        g = jnp.where(col == idx, -jnp.inf, g)
    es = [jnp.exp(vv - vals[0]) for vv in vals]
    den = es[0]
    for e in es[1:]:
        den = den + e
    gates = jnp.zeros(logits.shape, F32)
    experts = jnp.zeros(logits.shape, jnp.int32)
    for kk in range(TOP_K):
        gates = jnp.where(col == kk, es[kk] / den, gates)
        experts = jnp.where(col == kk, idxs[kk], experts)
    gate_ref[...] = gates
    exp_ref[...] = experts


def _outproj(oa, ob, x2d, w_out, g2, wr, br):
    n = x2d.shape[0]
    tm = ROW_TILE
    row = lambda w: pl.BlockSpec((tm, w), lambda i: (i, 0))
    full = lambda a: pl.BlockSpec(a.shape, lambda i: (0,) * a.ndim)
    return pl.pallas_call(
        _outproj_body,
        grid=(n // tm,),
        in_specs=[row(WIDTH), row(WIDTH), row(D_MODEL), full(w_out), full(g2), full(wr), full(br)],
        out_specs=(row(D_MODEL), row(D_MODEL), row(LANES), row(LANES)),
        out_shape=(jax.ShapeDtypeStruct((n, D_MODEL), F32), jax.ShapeDtypeStruct((n, D_MODEL), BF16),
                   jax.ShapeDtypeStruct((n, LANES), F32), jax.ShapeDtypeStruct((n, LANES), jnp.int32)),
        compiler_params=pltpu.CompilerParams(dimension_semantics=("parallel",),
                                             vmem_limit_bytes=VMEM_LIMIT),
        name="outproj",
    )(oa, ob, x2d, w_out, g2, wr, br)


def _moe_body(be_ref, nv_ref, x_ref, wg_ref, bg_ref, wu_ref, bu_ref, wd_ref, bdn_ref, y_ref,
              wg_s, wu_s, wd_s):
    i = pl.program_id(0)
    new_expert = jnp.logical_or(i == 0, be_ref[i] != be_ref[jnp.maximum(i - 1, 0)])

    @pl.when(new_expert)
    def _():
        wg_s[...] = wg_ref[0].astype(BF16)
        wu_s[...] = wu_ref[0].astype(BF16)
        wd_s[...] = wd_ref[0].astype(BF16)

    @pl.when(i < nv_ref[0])
    def _():
        x = x_ref[...]
        gate = jnp.minimum(_dot(x, wg_s[...]) + bg_ref[0], SWIGLU_LIMIT)
        up = jnp.clip(_dot(x, wu_s[...]) + bu_ref[0], -SWIGLU_LIMIT, SWIGLU_LIMIT)
        glu = gate * _sigmoid(SWIGLU_ALPHA * gate)
        hmid = ((up + 1.0) * glu).astype(BF16)
        y_ref[...] = _dot(hmid, wd_s[...]) + bdn_ref[0]

    @pl.when(i >= nv_ref[0])
    def _():
        y_ref[...] = jnp.zeros_like(y_ref)


def _moe(block_e, nvalid, xb, wg, bg, wu, bu, wd, bdn):
    n_rows = xb.shape[0]
    nb = n_rows // MOE_ROWS
    d_ff = wg.shape[2]
    wspec = lambda s: pl.BlockSpec((1,) + s, lambda i, be, nv: (be[i], 0, 0))
    grid_spec = pltpu.PrefetchScalarGridSpec(
        num_scalar_prefetch=2,
        grid=(nb,),
        in_specs=[pl.BlockSpec((MOE_ROWS, D_MODEL), lambda i, be, nv: (i, 0)),
                  wspec((D_MODEL, d_ff)), wspec((1, d_ff)), wspec((D_MODEL, d_ff)), wspec((1, d_ff)),
                  wspec((d_ff, D_MODEL)), wspec((1, D_MODEL))],
        out_specs=pl.BlockSpec((MOE_ROWS, D_MODEL), lambda i, be, nv: (i, 0)),
        scratch_shapes=[pltpu.VMEM((D_MODEL, d_ff), BF16), pltpu.VMEM((D_MODEL, d_ff), BF16),
                        pltpu.VMEM((d_ff, D_MODEL), BF16)],
    )
    return pl.pallas_call(
        _moe_body,
        grid_spec=grid_spec,
        out_shape=jax.ShapeDtypeStruct((n_rows, D_MODEL), F32),
        compiler_params=pltpu.CompilerParams(dimension_semantics=("arbitrary",),
                                             vmem_limit_bytes=VMEM_LIMIT),
        name="moe_ffn",
    )(block_e, nvalid, xb, wg, bg, wu, bu, wd, bdn)


def _bucket_np(dist):
    n = np.maximum(dist, 0)
    max_exact = NUM_BUCKETS // 2
    n_f = np.maximum(n, max_exact).astype(np.float32)
    large = max_exact + (np.log(n_f / np.float32(max_exact)) / np.float32(math.log(MAX_DISTANCE / max_exact))
                         * np.float32(NUM_BUCKETS - max_exact)).astype(np.int32)
    large = np.minimum(large, NUM_BUCKETS - 1)
    return np.where(n < max_exact, n, large).astype(np.int32)


def _blockdiag_state(s):
    b = s.shape[0]
    ng = N_HEADS // GROUP
    s5 = s.reshape(b, ng, GROUP, HEAD_DIM, HEAD_DIM)
    eye = jnp.eye(GROUP, dtype=s.dtype)
    return jnp.einsum("bghde,hk->bghdke", s5, eye).reshape(b, ng, GW, GW)


def _unblock_state(sbd):
    b = sbd.shape[0]
    ng = N_HEADS // GROUP
    s6 = sbd.reshape(b, ng, GROUP, HEAD_DIM, GROUP, HEAD_DIM)
    d = jnp.diagonal(s6, axis1=2, axis2=4)
    return jnp.moveaxis(d, -1, 2).reshape(b, N_HEADS, HEAD_DIM, HEAD_DIM)


def _mixer(x, cache_k, cache_v, page_table, s0, cbuf, rel_bias, lw):
    (g1, w_cat, bd, qg, kg, cw, av, dtb, eb, eg, og) = lw
    b, t, _ = x.shape
    x2d = x.reshape(b * t, D_MODEL)
    prompt = cache_k is None
    outs = _inproj(x2d, g1, w_cat, bd, qg, kg, seq_len=t, attn_layout=prompt)
    qkva, z, bg, qb, kb, vb = outs[:6]
    r3 = lambda a: a.reshape(b, t, a.shape[-1])
    qkva3 = r3(qkva)
    c = min(DELTA_CHUNK, t)
    tt = min(ROW_TILE, t)
    o_a, s_fin = _gdn(qkva3, r3(z), r3(bg), cbuf, _blockdiag_state(s0), cw, av, dtb, eb, eg, bd, og,
                      tt=tt, c=c)
    new_conv = jnp.concatenate([cbuf, qkva3], axis=1)[:, t:] if t < CONV_WIDTH - 1 else qkva3[:, t - (CONV_WIDTH - 1):]
    qb3, kb3, vb3 = r3(qb), r3(kb), r3(vb)
    rb = rel_bias.astype(F32)

    if prompt:
        kbf, vt, km = outs[6:]
        nb = t // MOBA_BLOCK
        kmp = jnp.pad(km.reshape(b, nb, WIDTH), ((0, 0), (0, LANES - nb), (0, 0)))
        ii = np.arange(MOBA_BLOCK)
        d_own = ii[None, :] - ii[:, None]
        bidx = np.stack([_bucket_np(d_own), _bucket_np(d_own + MOBA_BLOCK)])
        far_b = _bucket_np(np.arange(MOBA_BLOCK + 1, max(t, MOBA_BLOCK + 2)))
        assert (far_b == far_b[0]).all()
        bias_tab = jnp.transpose(rb[bidx], (3, 0, 1, 2))
        far = rb[int(far_b[0])]
        o_b = _moba_prompt(qb3, r3(kbf), vt, kmp, bias_tab, far).reshape(b * t, WIDTH)
    else:
        npages = page_table.shape[1]
        page = cache_k.shape[1]
        past = npages * page
        assert page == LANES and past % MOBA_BLOCK == 0 and t <= LANES
        n_past = past // MOBA_BLOCK
        pps = 8 if npages % 8 == 0 else 2
        rows = N_HEADS * t
        pt_flat = page_table.reshape(-1).astype(jnp.int32)
        slab = lambda cch: jnp.transpose(cch, (0, 2, 3, 1)).reshape(cch.shape[0], WIDTH, page)
        scores, kmeans = _dec_scores(pt_flat, qb3, slab(cache_k), npages=npages, pps=pps)
        kmp = jnp.pad(kmeans.reshape(b, n_past, WIDTH), ((0, 0), (0, LANES - n_past), (0, 0)))
        tpos = np.arange(t)
        d_past = past + tpos[:, None] - np.arange(past)[None, :]
        bias_past = jnp.transpose(rb[_bucket_np(d_past)], (2, 0, 1)).reshape(rows, past)
        d_own = np.zeros((t, LANES), np.int64)
        d_own[:, :t] = tpos[:, None] - tpos[None, :]
        bias_own = jnp.transpose(rb[_bucket_np(d_own)], (2, 0, 1)).reshape(rows, LANES)
        ex = np.zeros((LANES, past), np.float32)
        ex[np.arange(past) // MOBA_BLOCK, np.arange(past)] = 1.0
        probs, o_own = _dec_select(scores, kmp, qb3, kb3, vb3, bias_past, bias_own, jnp.asarray(ex, BF16),
                                   n_past=n_past)
        o_b = _dec_pv(pt_flat, probs, o_own, slab(cache_v), npages=npages, pps=pps, tq=t).reshape(b * t, WIDTH)
    return o_a.reshape(b * t, WIDTH), o_b, kb3, vb3, _unblock_state(s_fin), new_conv


def _moe_ffn(xn, gates, experts, x1, wg, bg, wu, bu, wd, bdn):
    n = xn.shape[0]
    nk = n * TOP_K
    flat_e = experts[:, :TOP_K].reshape(nk)
    flat_g = gates[:, :TOP_K].reshape(nk)
    onehot = (flat_e[:, None] == jnp.arange(N_EXPERTS, dtype=jnp.int32)[None, :]).astype(jnp.int32)
    csum = jnp.cumsum(onehot, axis=0)
    counts = csum[-1]
    rank = jnp.sum(csum * onehot, axis=1) - 1
    padded = ((counts + MOE_ROWS - 1) // MOE_ROWS) * MOE_ROWS
    pad_end = jnp.cumsum(padded)
    pad_start = pad_end - padded
    dest = pad_start[flat_e] + rank
    n_blocks = -(-nk // MOE_ROWS) + N_EXPERTS
    n_rows = n_blocks * MOE_ROWS
    flat_tok = jnp.arange(nk, dtype=jnp.int32) // TOP_K
    slot_tok = jnp.full((n_rows,), n, jnp.int32).at[dest].set(flat_tok, unique_indices=True)
    nvalid = (pad_end[-1] // MOE_ROWS).astype(jnp.int32).reshape(1)
    blk_start = jnp.arange(n_blocks, dtype=jnp.int32) * MOE_ROWS
    block_e = jnp.minimum(jnp.searchsorted(pad_end, blk_start, side="right"), N_EXPERTS - 1).astype(jnp.int32)
    last_e = block_e[jnp.maximum(nvalid[0] - 1, 0)]
    block_e = jnp.where(jnp.arange(n_blocks) < nvalid[0], block_e, last_e)
    x_ext = jnp.concatenate([xn, jnp.zeros((1, D_MODEL), xn.dtype)], axis=0)
    xb = x_ext[slot_tok]
    yb = _moe(block_e, nvalid, xb, wg, bg, wu, bu, wd, bdn)
    picked = yb[dest.reshape(n, TOP_K)]
    return x1 + jnp.sum(picked * flat_g.reshape(n, TOP_K, 1), axis=1)


def kernel(x_prompt, x_sample, cache_k, cache_v, state_delta, state_conv, page_table, rel_bias,
           norm1_g, w_in, conv_w, A_log, dt_bias, o_norm_g, q_norm_g, k_norm_g, w_out, norm2_g,
           w_router, b_router, w_gate, b_gate, w_up, b_up, w_down, b_down):
    depth = norm1_g.shape[0]
    bp, tp, _ = x_prompt.shape
    bs, ts, _ = x_sample.shape
    yp, ys = x_prompt, x_sample
    outs = [[] for _ in range(8)]
    seg = np.arange(WIDTH) // HEAD_DIM
    segw = np.arange(GW) // HEAD_DIM
    bd = jnp.asarray((segw[:, None] == segw[None, :]).astype(np.float32) / HEAD_DIM, BF16)
    eb_np = np.zeros((LANES, WIDTH), np.float32)
    eb_np[seg, np.arange(WIDTH)] = 1.0
    eg_np = np.zeros((LANES, WIDTH), np.float32)
    eg_np[N_HEADS + seg, np.arange(WIDTH)] = 1.0
    eb, eg = jnp.asarray(eb_np, BF16), jnp.asarray(eg_np, BF16)
    c1, c3 = QKV + WIDTH, QKV + WIDTH + 2 * N_HEADS
    for l in range(depth):
        wl = w_in[l]
        w_cat = jnp.concatenate([wl[:, :c1], wl[:, c3:], wl[:, c1:c3],
                                 jnp.zeros((D_MODEL, LANES - 2 * N_HEADS), wl.dtype)], axis=1).astype(BF16)
        tile8 = lambda g: jnp.tile(g.astype(F32), N_HEADS).reshape(1, WIDTH)
        av = jnp.zeros((1, LANES), F32).at[0, N_HEADS:2 * N_HEADS].set(-jnp.exp(A_log[l].astype(F32)))
        dtb = jnp.zeros((1, LANES), F32).at[0, N_HEADS:2 * N_HEADS].set(dt_bias[l].astype(F32))
        lw = (norm1_g[l].astype(F32).reshape(1, D_MODEL), w_cat, bd, tile8(q_norm_g[l]), tile8(k_norm_g[l]),
              conv_w[l].astype(F32), av, dtb, eb, eg, tile8(o_norm_g[l]))
        s0_p = jnp.zeros((bp, N_HEADS, HEAD_DIM, HEAD_DIM), F32)
        c0_p = jnp.zeros((bp, CONV_WIDTH - 1, QKV), F32)
        oa_p, ob_p, k_p, v_p, s_p, c_p = _mixer(yp, None, None, None, s0_p, c0_p, rel_bias, lw)
        oa_s, ob_s, k_s, v_s, s_s, c_s = _mixer(ys, cache_k[l], cache_v[l], page_table, state_delta[l],
                                                state_conv[l], rel_bias, lw)
        w_o = w_out[l].astype(BF16)
        g2 = norm2_g[l].astype(F32).reshape(1, D_MODEL)
        wr = jnp.pad(w_router[l].astype(F32), ((0, 0), (0, LANES - N_EXPERTS)))
        br = jnp.pad(b_router[l].astype(F32), (0, LANES - N_EXPERTS)).reshape(1, LANES)
        x1_p, xn_p, gt_p, ex_p = _outproj(oa_p, ob_p, yp.reshape(bp * tp, D_MODEL), w_o, g2, wr, br)
        x1_s, xn_s, gt_s, ex_s = _outproj(oa_s, ob_s, ys.reshape(bs * ts, D_MODEL), w_o, g2, wr, br)
        cat = lambda a, b_: jnp.concatenate([a, b_], axis=0)
        y_all = _moe_ffn(cat(xn_p, xn_s), cat(gt_p, gt_s), cat(ex_p, ex_s), cat(x1_p, x1_s),
                         w_gate[l].astype(F32), b_gate[l].astype(F32)[:, None, :],
                         w_up[l].astype(F32), b_up[l].astype(F32)[:, None, :],
                         w_down[l].astype(F32), b_down[l].astype(F32)[:, None, :])
        yp = y_all[:bp * tp].reshape(bp, tp, D_MODEL)
        ys = y_all[bp * tp:].reshape(bs, ts, D_MODEL)
        shp = lambda a, b_, t_: a.reshape(b_, t_, N_HEADS, HEAD_DIM)
        for lst, val in zip(outs, (shp(k_p, bp, tp), shp(v_p, bp, tp), s_p, c_p,
                                   shp(k_s, bs, ts), shp(v_s, bs, ts), s_s, c_s)):
            lst.append(val)
    stacked = [jnp.stack(o) for o in outs]
    return (yp, ys, *stacked)
```

```python
import functools
import math

import numpy as np
import jax
import jax.numpy as jnp
from jax import lax
from jax.experimental import pallas as pl
from jax.experimental.pallas import tpu as pltpu

F32 = jnp.float32
BF16 = jnp.bfloat16

D_MODEL = 1024
N_HEADS = 8
HEAD_DIM = 64
WIDTH = N_HEADS * HEAD_DIM
QKV = 3 * WIDTH
CONV_WIDTH = 4
DELTA_CHUNK = 64
MOBA_BLOCK = 256
MOBA_TOP = 3
ATTN_SCALE = HEAD_DIM ** -0.5
NUM_BUCKETS = 32
MAX_DISTANCE = 128
N_EXPERTS = 32
TOP_K = 4
SWIGLU_LIMIT = 7.0
SWIGLU_ALPHA = 1.702
RMS_EPS = 1e-6

LANES = 128
SUBLANES = 8
ROW_TILE = 256
MOE_ROWS = 256
NEG = -1e30
VMEM_LIMIT = 48 * 1024 * 1024

GROUP = 4
GW = GROUP * HEAD_DIM
NPAIR = WIDTH // LANES
FAR_BLOCKS = 4

_NN = (((1,), (0,)), ((), ()))
_NT = (((1,), (1,)), ((), ()))
_TN = (((0,), (0,)), ((), ()))


def _dot(a, b, dims=_NN):
    return lax.dot_general(a, b, dims, preferred_element_type=F32)


def _split2(a):
    hi = a.astype(BF16)
    lo = (a - hi.astype(F32)).astype(BF16)
    return hi, lo


def _split3(a):
    hi = a.astype(BF16)
    r = a - hi.astype(F32)
    mid = r.astype(BF16)
    lo = (r - mid.astype(F32)).astype(BF16)
    return hi, mid, lo


def _mm1(a, b, dims=_NN):
    return _dot(a.astype(BF16), b.astype(BF16), dims)


def _mm3(a, b, dims=_NN):
    ah, al = _split2(a)
    bh, bl = _split2(b)
    return _dot(ah, bh, dims) + (_dot(ah, bl, dims) + _dot(al, bh, dims))


def _mm_exact_rhs(a, b_bf16, dims=_NN):
    hi, mid, lo = _split3(a)
    return _dot(hi, b_bf16, dims) + (_dot(mid, b_bf16, dims) + _dot(lo, b_bf16, dims))


def _mm_exact_lhs(a_bf16, b, dims=_NN):
    hi, mid, lo = _split3(b)
    return _dot(a_bf16, hi, dims) + (_dot(a_bf16, mid, dims) + _dot(a_bf16, lo, dims))


def _seg_mean(x, bd):
    outs = []
    for g in range(x.shape[1] // GW):
        hi, lo = _split2(x[:, g * GW:(g + 1) * GW])
        outs.append(_dot(hi, bd) + _dot(lo, bd))
    return outs[0] if len(outs) == 1 else jnp.concatenate(outs, axis=1)


def _sigmoid(x):
    return 1.0 / (1.0 + jnp.exp(-x))


def _softplus(x):
    return jnp.maximum(x, 0.0) + jnp.log(1.0 + jnp.exp(-jnp.abs(x)))


def _inproj_body(x_ref, g_ref, w_ref, bd_ref, qg_ref, kg_ref,
                 qkva_ref, z_ref, bg_ref, qb_ref, kb_ref, vb_ref, *attn_refs):
    x = x_ref[...]
    ms = jnp.mean(x * x, axis=-1, keepdims=True)
    h = (x * lax.rsqrt(ms + RMS_EPS) * g_ref[...]).astype(BF16)

    def proj(lo, hi):
        return _dot(h, w_ref[:, lo:hi])

    bd = bd_ref[...]
    qkva_ref[...] = proj(0, QKV)
    z_ref[...] = proj(QKV, QKV + WIDTH)
    c = QKV + WIDTH
    q = proj(c, c + WIDTH)
    qb_ref[...] = q * lax.rsqrt(_seg_mean(q * q, bd) + RMS_EPS) * qg_ref[...]
    k = proj(c + WIDTH, c + 2 * WIDTH)
    kn = k * lax.rsqrt(_seg_mean(k * k, bd) + RMS_EPS) * kg_ref[...]
    v = proj(c + 2 * WIDTH, c + 3 * WIDTH)
    bg_ref[...] = proj(c + 3 * WIDTH, c + 3 * WIDTH + LANES)
    if attn_refs:
        kbf_ref, vt_ref, km_ref = attn_refs
        kb_ref[0] = kn.T
        vt = v.T
        vb_ref[0] = vt
        kbf_ref[...] = kn.astype(BF16)
        km_ref[0] = jnp.mean(kn, axis=0, keepdims=True)
        vtb = vt.astype(BF16)
        for pp in range(NPAIR):
            vt_ref[0, pp, 0] = vtb[pp * LANES:(pp + 1) * LANES, :]
    else:
        kb_ref[...] = kn
        vb_ref[...] = v


def _inproj(x2d, g1, w_cat, bd, qg, kg, *, seq_len, attn_layout):
    n = x2d.shape[0]
    tm = ROW_TILE
    nt = n // tm
    row = lambda w: pl.BlockSpec((tm, w), lambda i: (i, 0))
    full = lambda a: pl.BlockSpec(a.shape, lambda i: (0,) * a.ndim)
    out_shape = [
        jax.ShapeDtypeStruct((n, QKV), F32), jax.ShapeDtypeStruct((n, WIDTH), F32),
        jax.ShapeDtypeStruct((n, LANES), F32), jax.ShapeDtypeStruct((n, WIDTH), F32),
    ]
    out_specs = [row(QKV), row(WIDTH), row(LANES), row(WIDTH)]
    if not attn_layout:
        out_shape += [jax.ShapeDtypeStruct((n, WIDTH), F32)] * 2
        out_specs += [row(WIDTH)] * 2
    else:
        tpb = seq_len // tm
        kv_t = jax.ShapeDtypeStruct((n // seq_len, WIDTH, seq_len), F32)
        kv_spec = pl.BlockSpec((1, WIDTH, tm), lambda i: (i // tpb, 0, i % tpb))
        out_shape += [kv_t, kv_t]
        out_specs += [kv_spec, kv_spec]
        out_shape += [jax.ShapeDtypeStruct((n, WIDTH), BF16),
                      jax.ShapeDtypeStruct((n // seq_len, NPAIR, tpb, LANES, tm), BF16),
                      jax.ShapeDtypeStruct((nt, 1, WIDTH), F32)]
        out_specs += [row(WIDTH),
                      pl.BlockSpec((1, NPAIR, 1, LANES, tm), lambda i: (i // tpb, 0, i % tpb, 0, 0)),
                      pl.BlockSpec((1, 1, WIDTH), lambda i: (i, 0, 0))]
    return pl.pallas_call(
        _inproj_body,
        grid=(nt,),
        in_specs=[row(D_MODEL), full(g1), full(w_cat), full(bd), full(qg), full(kg)],
        out_specs=tuple(out_specs),
        out_shape=tuple(out_shape),
        compiler_params=pltpu.CompilerParams(dimension_semantics=("parallel",),
                                             vmem_limit_bytes=VMEM_LIMIT),
        name="inproj",
    )(x2d, g1, w_cat, bd, qg, kg)


def _stack(x, masks):
    return jnp.concatenate([jnp.where(m, x, 0.0) for m in masks], axis=0)


def _unstack(w, c, n):
    acc = w[0:c]
    for h in range(1, n):
        acc = acc + w[h * c:(h + 1) * c]
    return acc


def _gdn_body(qkva_ref, z_ref, bg_ref, cbuf_ref, s0_ref, cw_ref, av_ref, dtb_ref, eb_ref, eg_ref,
              bd_ref, og_ref, o_ref, sfin_ref, s_scr, carry_scr, uext_scr, *, tt, c):
    t = pl.program_id(1)
    nt = pl.num_programs(1)
    gc = GROUP * c

    @pl.when(t == 0)
    def _():
        s_scr[...] = s0_ref[0]
        carry_scr[...] = jnp.zeros_like(carry_scr)
        carry_scr[SUBLANES - (CONV_WIDTH - 1):SUBLANES, :] = cbuf_ref[0]

    uext_scr[0:SUBLANES, :] = carry_scr[...]
    uext_scr[SUBLANES:SUBLANES + tt, :] = qkva_ref[0]
    carry_scr[...] = uext_scr[tt:tt + SUBLANES, :]

    cw = cw_ref[...]
    off = SUBLANES - (CONV_WIDTH - 1)
    conv = cw[0:1] * uext_scr[off:off + tt, :]
    for i in range(1, CONV_WIDTH):
        conv = conv + cw[i:i + 1] * uext_scr[off + i:off + i + tt, :]
    act = conv * _sigmoid(conv)

    bd = bd_ref[...]
    q = act[:, 0:WIDTH]
    k = act[:, WIDTH:2 * WIDTH]
    v = act[:, 2 * WIDTH:3 * WIDTH]
    q = q * lax.rsqrt(_seg_mean(q * q, bd) * HEAD_DIM + RMS_EPS) * (HEAD_DIM ** -0.5)
    k = k * lax.rsqrt(_seg_mean(k * k, bd) * HEAD_DIM + RMS_EPS)

    bg = bg_ref[0]
    beta = _mm_exact_rhs(_sigmoid(bg), eb_ref[...])
    gx = _mm_exact_rhs(av_ref[...] * _softplus(bg + dtb_ref[...]), eg_ref[...])

    r4 = lax.broadcasted_iota(jnp.int32, (gc, gc), 0)
    c4 = lax.broadcasted_iota(jnp.int32, (gc, gc), 1)
    same = (r4 // c) == (c4 // c)
    tri = jnp.logical_and(same, r4 >= c4)
    strict = jnp.logical_and(same, r4 > c4)
    eye = (r4 == c4).astype(F32)
    rc = lax.broadcasted_iota(jnp.int32, (c, c), 0)
    cc = lax.broadcasted_iota(jnp.int32, (c, c), 1)
    tril_c = (rc >= cc).astype(BF16)
    lane_g = lax.broadcasted_iota(jnp.int32, (1, GW), 1) // HEAD_DIM
    hmasks = [lane_g == h for h in range(GROUP)]
    rs = lax.broadcasted_iota(jnp.int32, (GW, GW), 0) // HEAD_DIM
    cs = lax.broadcasted_iota(jnp.int32, (GW, GW), 1) // HEAD_DIM
    bdm = (rs == cs).astype(F32)
    nsq = int(round(math.log2(c))) - 1

    nch, ngr = tt // c, N_HEADS // GROUP
    inst = []
    for ch in range(nch):
        r0 = ch * c
        gcum = _mm_exact_lhs(tril_c, gx[r0:r0 + c])
        eg = jnp.exp(gcum)
        glast = gcum[c - 1:c, :]
        kscale = jnp.exp(glast - gcum)
        gtot = jnp.exp(glast)
        for gr in range(ngr):
            sl = slice(gr * GW, (gr + 1) * GW)
            kq = k[r0:r0 + c, sl]
            qq = q[r0:r0 + c, sl]
            bb = beta[r0:r0 + c, sl]
            gq = gcum[:, sl]
            kb = kq * bb
            xk = _stack(kq, hmasks).astype(BF16)
            kk = _dot(_stack(kb, hmasks).astype(BF16), xk, _NT)
            qk = _dot(_stack(qq, hmasks).astype(BF16), xk, _NT)
            gcol = jnp.concatenate(
                [jnp.broadcast_to(gq[:, h * HEAD_DIM:h * HEAD_DIM + 1], (c, gc)) for h in range(GROUP)], axis=0)
            if gc % LANES == 0:
                grow = gcol.T
            else:
                grow = _mm_exact_lhs(jnp.full((gc, GW), 1.0 / HEAD_DIM, BF16), _stack(gq, hmasks), _NT)
            dm = jnp.where(tri, jnp.exp(jnp.where(tri, gcol - grow, 0.0)), 0.0)
            lmat = jnp.where(strict, kk * dm, 0.0)
            rhs = jnp.concatenate([_stack(v[r0:r0 + c, sl] * bb, hmasks), _stack(kb * eg[:, sl], hmasks)], axis=1)
            inst.append(dict(gr=gr, sl=sl, amat=qk * dm, pinv=eye - lmat, msq=lmat, rhs=rhs,
                             qe=qq * eg[:, sl], kd=kq * kscale[:, sl], gtot=gtot[:, sl]))
    for _ in range(nsq):
        for it in inst:
            it["msq"] = _mm1(it["msq"], it["msq"])
        for it in inst:
            it["pinv"] = it["pinv"] + _mm1(it["pinv"], it["msq"])
    for it in inst:
        w = _mm1(it["pinv"], it["rhs"])
        it["value"] = _unstack(w[:, 0:GW], c, GROUP)
        it["kcum"] = _unstack(w[:, GW:2 * GW], c, GROUP)
    o_rows = []
    for ch in range(nch):
        o_groups = []
        for it in inst[ch * ngr:(ch + 1) * ngr]:
            s = s_scr[it["gr"]]
            u = it["value"] - _mm3(it["kcum"], s)
            o_groups.append(_mm1(it["qe"], s) + _unstack(_mm1(it["amat"], _stack(u, hmasks)), c, GROUP))
            s_scr[it["gr"]] = s * it["gtot"] + bdm * _mm3(it["kd"], u, _TN)
        o_rows.append(jnp.concatenate(o_groups, axis=1))
    o = o_rows[0] if len(o_rows) == 1 else jnp.concatenate(o_rows, axis=0)
    on = o * lax.rsqrt(_seg_mean(o * o, bd) + RMS_EPS) * og_ref[...]
    z = z_ref[0]
    o_ref[0] = on * (z * _sigmoid(z))

    @pl.when(t == nt - 1)
    def _():
        sfin_ref[0] = s_scr[...]


def _gdn(qkva, z, bg, cbuf, s0bd, cw, av, dtb, eb, eg, bd, og, *, tt, c):
    b, t, _ = qkva.shape
    ng = N_HEADS // GROUP
    full = lambda a: pl.BlockSpec(a.shape, lambda i, j: (0,) * a.ndim)
    body = functools.partial(_gdn_body, tt=tt, c=c)
    return pl.pallas_call(
        body,
        grid=(b, t // tt),
        in_specs=[
            pl.BlockSpec((1, tt, QKV), lambda i, j: (i, j, 0)),
            pl.BlockSpec((1, tt, WIDTH), lambda i, j: (i, j, 0)),
            pl.BlockSpec((1, tt, LANES), lambda i, j: (i, j, 0)),
            pl.BlockSpec((1, CONV_WIDTH - 1, QKV), lambda i, j: (i, 0, 0)),
            pl.BlockSpec((1, ng, GW, GW), lambda i, j: (i, 0, 0, 0)),
            full(cw), full(av), full(dtb), full(eb), full(eg), full(bd), full(og),
        ],
        out_specs=(pl.BlockSpec((1, tt, WIDTH), lambda i, j: (i, j, 0)),
                   pl.BlockSpec((1, ng, GW, GW), lambda i, j: (i, 0, 0, 0))),
        out_shape=(jax.ShapeDtypeStruct((b, t, WIDTH), F32),
                   jax.ShapeDtypeStruct((b, ng, GW, GW), F32)),
        scratch_shapes=[pltpu.VMEM((ng, GW, GW), F32), pltpu.VMEM((SUBLANES, QKV), F32),
                        pltpu.VMEM((tt + SUBLANES, QKV), F32)],
        compiler_params=pltpu.CompilerParams(dimension_semantics=("arbitrary", "arbitrary"),
                                             vmem_limit_bytes=VMEM_LIMIT),
        name="gdn",
    )(qkva, z, bg, cbuf, s0bd, cw, av, dtb, eb, eg, bd, og)


def _top_select(gate, idx, n_cand, n_valid_f, axis):
    g = jnp.where(idx < n_cand, gate, -jnp.inf)
    sel = jnp.zeros(gate.shape, F32)
    for r in range(MOBA_TOP):
        mx = jnp.max(g, axis=axis, keepdims=True)
        first = jnp.min(jnp.where(g == mx, idx, jnp.int32(1 << 30)), axis=axis, keepdims=True)
        hit = idx == first
        sel = jnp.maximum(sel, jnp.where(hit, n_valid_f[r], 0.0))
        g = jnp.where(hit, -jnp.inf, g)
    return sel


def _moba_prompt_body(far_ref, q_ref, k_ref, vt_ref, km_ref, bias_ref, o_ref, sel_scr):
    p = pl.program_id(1)
    i = pl.program_id(2)
    blk = MOBA_BLOCK
    qt = (q_ref[0] * ATTN_SCALE).T
    row = lax.broadcasted_iota(jnp.int32, (LANES, blk), 0)
    top = row < HEAD_DIM
    qt_m = [jnp.where(top, qt, 0.0), jnp.where(top, 0.0, qt)]
    qtb = [x.astype(BF16) for x in qt_m]
    km = km_ref[0]
    n_valid = [(i > r).astype(F32) for r in range(MOBA_TOP)]
    for hh in range(2):
        sel_scr[hh] = _top_select(_mm3(km, qt_m[hh]), row, i, n_valid, 0)
    rowk = lax.broadcasted_iota(jnp.int32, (blk, blk), 0)
    colq = lax.broadcasted_iota(jnp.int32, (blk, blk), 1)

    def step(n, carry, modes):
        m0, l0, m1, l1, acc = carry
        ms, ls = [m0, m1], [l0, l1]
        nblk = len(modes)
        kblks = [k_ref[0, pl.ds(pl.multiple_of((n + j) * blk, blk), blk), :] for j in range(nblk)]
        vts = [vt_ref[0, 0, n + j] for j in range(nblk)]
        alphas, pvs = [], []
        for hh in range(2):
            ss, shifts, valids, bmax = [], [], [], []
            for j, mode in enumerate(modes):
                s = _dot(kblks[j], qtb[hh])
                if mode == "own":
                    s = jnp.where(rowk <= colq, s + bias_ref[hh, 0], NEG)
                    shift, valid = 0.0, None
                else:
                    if mode == "prev":
                        s, shift = s + bias_ref[hh, 1], 0.0
                    else:
                        shift = far_ref[2 * p + hh]
                    valid = sel_scr[hh, pl.ds(n + j, 1), :] > 0.5
                bm = jnp.max(s, axis=0, keepdims=True) + shift
                ss.append(s)
                shifts.append(shift)
                valids.append(valid)
                bmax.append(bm if valid is None else jnp.where(valid, bm, NEG))
            m_new = ms[hh]
            for bm in bmax:
                m_new = jnp.maximum(m_new, bm)
            alpha = jnp.exp(ms[hh] - m_new)
            lsum = alpha * ls[hh]
            pv = None
            for j, s in enumerate(ss):
                off = m_new - shifts[j]
                if valids[j] is not None:
                    off = jnp.where(valids[j], off, -NEG)
                pr = jnp.exp(s - off)
                lsum = lsum + jnp.sum(pr, axis=0, keepdims=True)
                d = _dot(vts[j], pr.astype(BF16))
                pv = d if pv is None else pv + d
            ls[hh] = lsum
            ms[hh] = m_new
            alphas.append(alpha)
            pvs.append(pv)
        acc = acc * jnp.where(top, alphas[0], alphas[1]) + jnp.where(top, pvs[0], pvs[1])
        return ms[0], ls[0], ms[1], ls[1], acc

    init = (jnp.full((1, blk), NEG, F32), jnp.zeros((1, blk), F32),
            jnp.full((1, blk), NEG, F32), jnp.zeros((1, blk), F32),
            jnp.zeros((LANES, blk), F32))
    n_far = jnp.maximum(i - 1, 0)
    n_wide = n_far // FAR_BLOCKS
    carry = lax.fori_loop(0, n_wide, lambda w, cr: step(w * FAR_BLOCKS, cr, ("far",) * FAR_BLOCKS), init)
    done = n_wide * FAR_BLOCKS
    width = FAR_BLOCKS // 2
    while width >= 1:
        take = ((n_far - done) >= width).astype(jnp.int32)
        carry = lax.fori_loop(0, take, lambda _, cr, d=done, w=width: step(d, cr, ("far",) * w), carry)
        done = done + take * width
        width //= 2
    has_prev = (i >= 1).astype(jnp.int32)
    carry = lax.fori_loop(0, has_prev, lambda _, cr: step(i - 1, cr, ("prev", "own")), carry)
    m0, l0, m1, l1, acc = lax.fori_loop(0, 1 - has_prev, lambda _, cr: step(i, cr, ("own",)), carry)
    o_ref[0] = (acc / jnp.where(top, l0, l1)).T


def _moba_prompt(qb, kbf, vt, kmp, bias_tab, far):
    b, t, _ = qb.shape
    nq = t // MOBA_BLOCK
    grid_spec = pltpu.PrefetchScalarGridSpec(
        num_scalar_prefetch=1,
        grid=(b, NPAIR, nq),
        in_specs=[
            pl.BlockSpec((1, MOBA_BLOCK, LANES), lambda bi, p, i, far: (bi, i, p)),
            pl.BlockSpec((1, t, LANES), lambda bi, p, i, far: (bi, 0, p)),
            pl.BlockSpec((1, 1, nq, LANES, MOBA_BLOCK), lambda bi, p, i, far: (bi, p, 0, 0, 0)),
            pl.BlockSpec((1, LANES, LANES), lambda bi, p, i, far: (bi, 0, p)),
            pl.BlockSpec((2, 2, MOBA_BLOCK, MOBA_BLOCK), lambda bi, p, i, far: (p, 0, 0, 0)),
        ],
        out_specs=pl.BlockSpec((1, MOBA_BLOCK, LANES), lambda bi, p, i, far: (bi, i, p)),
        scratch_shapes=[pltpu.VMEM((2, LANES, MOBA_BLOCK), F32)],
    )
    return pl.pallas_call(
        _moba_prompt_body,
        grid_spec=grid_spec,
        out_shape=jax.ShapeDtypeStruct((b, t, WIDTH), F32),
        compiler_params=pltpu.CompilerParams(
            dimension_semantics=("arbitrary", "arbitrary", "arbitrary"),
            vmem_limit_bytes=VMEM_LIMIT),
        name="moba_prompt",
    )(far, qb, kbf, vt, kmp, bias_tab)


def _head_masks(width):
    lane_h = lax.broadcasted_iota(jnp.int32, (1, width), 1) // HEAD_DIM
    return [lane_h == h for h in range(width // HEAD_DIM)]


def _dec_scores_body(pt_ref, q_ref, *refs, pps):
    k_refs = refs[:pps]
    s_ref, kmt_ref = refs[pps], refs[pps + 1]
    masks = _head_masks(WIDTH)
    qs = _stack(q_ref[0] * ATTN_SCALE, masks).astype(BF16)
    ppb = MOBA_BLOCK // LANES
    j = pl.program_id(1)
    for i in range(pps):
        s_ref[0, :, i * LANES:(i + 1) * LANES] = _dot(qs, k_refs[i][0].astype(BF16))
    lane = lax.broadcasted_iota(jnp.int32, (1, LANES), 1)
    @pl.when(j == 0)
    def _():
        kmt_ref[0] = jnp.zeros((WIDTH, LANES), F32)

    kmt = kmt_ref[0]
    for blk in range(pps // ppb):
        tot = k_refs[blk * ppb][0]
        for jj in range(1, ppb):
            tot = tot + k_refs[blk * ppb + jj][0]
        mean = jnp.sum(tot, axis=1, keepdims=True) * (1.0 / MOBA_BLOCK)
        kmt = jnp.where(lane == j * (pps // ppb) + blk, mean, kmt)
    kmt_ref[0] = kmt


def _page_specs(pps, npages):
    def mk(i):
        return pl.BlockSpec((1, WIDTH, LANES), lambda b, j, pt: (pt[b * npages + j * pps + i], 0, 0))
    return [mk(i) for i in range(pps)]


def _dec_scores(pt_flat, q, cache, *, npages, pps):
    b, tq, _ = q.shape
    ppb = MOBA_BLOCK // LANES
    nj = npages // pps
    grid_spec = pltpu.PrefetchScalarGridSpec(
        num_scalar_prefetch=1,
        grid=(b, nj),
        in_specs=[pl.BlockSpec((1, tq, WIDTH), lambda bi, j, pt: (bi, 0, 0))] + _page_specs(pps, npages),
        out_specs=(pl.BlockSpec((1, N_HEADS * tq, pps * LANES), lambda bi, j, pt: (bi, 0, j)),
                   pl.BlockSpec((1, WIDTH, LANES), lambda bi, j, pt: (bi, 0, 0))),
    )
    assert npages // ppb <= LANES
    return pl.pallas_call(
        functools.partial(_dec_scores_body, pps=pps),
        grid_spec=grid_spec,
        out_shape=(jax.ShapeDtypeStruct((b, N_HEADS * tq, npages * LANES), F32),
                   jax.ShapeDtypeStruct((b, WIDTH, LANES), F32)),
        compiler_params=pltpu.CompilerParams(dimension_semantics=("arbitrary", "arbitrary"),
                                             vmem_limit_bytes=VMEM_LIMIT),
        name="dec_scores",
    )(pt_flat, q, *([cache] * pps))


def _dec_select_body(s_ref, km_ref, q_ref, kn_ref, vn_ref, bp_ref, bo_ref, ex_ref, p_ref, oo_ref,
                     *, tq, n_past):
    rows = N_HEADS * tq
    masks = _head_masks(WIDTH)
    qs = _stack(q_ref[0] * ATTN_SCALE, masks)
    col = lax.broadcasted_iota(jnp.int32, (rows, LANES), 1)
    n_valid = [jnp.float32(1.0 if n_past > r else 0.0) for r in range(MOBA_TOP)]
    sel = _top_select(_mm3(qs, km_ref[0]), col, n_past, n_valid, -1)
    selk = _dot(sel.astype(BF16), ex_ref[...])
    logit = jnp.where(selk > 0.5, s_ref[0] + bp_ref[...], NEG)
    pad = jnp.zeros((LANES - tq, WIDTH), F32)
    kn = jnp.concatenate([kn_ref[0], pad], axis=0).astype(BF16)
    vn = jnp.concatenate([vn_ref[0], pad], axis=0).astype(BF16)
    trow = lax.rem(lax.broadcasted_iota(jnp.int32, (rows, LANES), 0), tq)
    s_own = jnp.where(col <= trow, _dot(qs.astype(BF16), kn, _NT) + bo_ref[...], NEG)
    m = jnp.maximum(jnp.max(logit, axis=-1, keepdims=True), jnp.max(s_own, axis=-1, keepdims=True))
    pr = jnp.exp(logit - m)
    po = jnp.exp(s_own - m)
    inv = 1.0 / (jnp.sum(pr, axis=-1, keepdims=True) + jnp.sum(po, axis=-1, keepdims=True))
    p_ref[0] = (pr * inv).astype(BF16)
    oo_ref[0] = _dot((po * inv).astype(BF16), vn)


def _dec_select(scores, kmp, q, kn, vn, bias_past, bias_own, expand, *, n_past):
    b, tq, _ = q.shape
    rows = N_HEADS * tq
    plen = scores.shape[2]
    full = lambda a: pl.BlockSpec(a.shape, lambda i: (0,) * a.ndim)
    per = lambda a: pl.BlockSpec((1,) + a.shape[1:], lambda i: (i,) + (0,) * (a.ndim - 1))
    return pl.pallas_call(
        functools.partial(_dec_select_body, tq=tq, n_past=n_past),
        grid=(b,),
        in_specs=[per(scores), per(kmp), per(q), per(kn), per(vn), full(bias_past), full(bias_own),
                  full(expand)],
        out_specs=(pl.BlockSpec((1, rows, plen), lambda i: (i, 0, 0)),
                   pl.BlockSpec((1, rows, WIDTH), lambda i: (i, 0, 0))),
        out_shape=(jax.ShapeDtypeStruct((b, rows, plen), BF16),
                   jax.ShapeDtypeStruct((b, rows, WIDTH), F32)),
        compiler_params=pltpu.CompilerParams(dimension_semantics=("arbitrary",),
                                             vmem_limit_bytes=VMEM_LIMIT),
        name="dec_select",
    )(scores, kmp, q, kn, vn, bias_past, bias_own, expand)


def _dec_pv_body(pt_ref, p_ref, oo_ref, *refs, pps, tq):
    v_refs = refs[:pps]
    o_ref, acc = refs[pps], refs[pps + 1]
    j = pl.program_id(1)

    @pl.when(j == 0)
    def _():
        acc[...] = oo_ref[0]

    tot = acc[...]
    for i in range(pps):
        tot = tot + _dot(p_ref[0, :, i * LANES:(i + 1) * LANES], v_refs[i][0].astype(BF16), _NT)
    acc[...] = tot

    @pl.when(j == pl.num_programs(1) - 1)
    def _():
        masks = _head_masks(WIDTH)
        a = acc[...]
        out = jnp.where(masks[0], a[0:tq], 0.0)
        for h in range(1, N_HEADS):
            out = out + jnp.where(masks[h], a[h * tq:(h + 1) * tq], 0.0)
        o_ref[0] = out


def _dec_pv(pt_flat, probs, o_own, cache, *, npages, pps, tq):
    b, rows, _ = probs.shape
    grid_spec = pltpu.PrefetchScalarGridSpec(
        num_scalar_prefetch=1,
        grid=(b, npages // pps),
        in_specs=[pl.BlockSpec((1, rows, pps * LANES), lambda bi, j, pt: (bi, 0, j)),
                  pl.BlockSpec((1, rows, WIDTH), lambda bi, j, pt: (bi, 0, 0))] + _page_specs(pps, npages),
        out_specs=pl.BlockSpec((1, tq, WIDTH), lambda bi, j, pt: (bi, 0, 0)),
        scratch_shapes=[pltpu.VMEM((rows, WIDTH), F32)],
    )
    return pl.pallas_call(
        functools.partial(_dec_pv_body, pps=pps, tq=tq),
        grid_spec=grid_spec,
        out_shape=jax.ShapeDtypeStruct((b, tq, WIDTH), F32),
        compiler_params=pltpu.CompilerParams(dimension_semantics=("arbitrary", "arbitrary"),
                                             vmem_limit_bytes=VMEM_LIMIT),
        name="dec_pv",
    )(pt_flat, probs, o_own, *([cache] * pps))


def _outproj_body(oa_ref, ob_ref, x_ref, w_ref, g_ref, wr_ref, br_ref, cnt0_ref,
                  x1_ref, xn_ref, gate_ref, exp_ref, cnt_ref, cnt_scr):
    @pl.when(pl.program_id(0) == 0)
    def _():
        cnt_scr[...] = cnt0_ref[...]

    mixed = _dot(oa_ref[...].astype(BF16), w_ref[0:WIDTH, :]) + _dot(ob_ref[...].astype(BF16),
                                                                      w_ref[WIDTH:2 * WIDTH, :])
    x1 = x_ref[...] + mixed
    x1_ref[...] = x1
    ms = jnp.mean(x1 * x1, axis=-1, keepdims=True)
    xn = x1 * lax.rsqrt(ms + RMS_EPS) * g_ref[...]
    xn_ref[...] = xn.astype(BF16)
    logits = _mm3(xn, wr_ref[...]) + br_ref[...]
    col = lax.broadcasted_iota(jnp.int32, logits.shape, 1)
    g = jnp.where(col < N_EXPERTS, logits, -jnp.inf)
    vals, idxs = [], []
    for _ in range(TOP_K):
        mx = jnp.max(g, axis=-1, keepdims=True)
        idx = jnp.min(jnp.where(g == mx, col, jnp.int32(1 << 30)), axis=-1, keepdims=True)
        vals.append(mx)
        idxs.append(idx)
        g = jnp.where(col == idx, -jnp.inf, g)
    es = [jnp.exp(vv - vals[0]) for vv in vals]
    den = es[0]
    for e in es[1:]:
        den = den + e
    tm = logits.shape[0]
    onehot = jnp.zeros(logits.shape, F32)
    for kk in range(TOP_K):
        onehot = onehot + (col == idxs[kk]).astype(F32)
    r_i = lax.broadcasted_iota(jnp.int32, (tm, tm), 0)
    c_i = lax.broadcasted_iota(jnp.int32, (tm, tm), 1)
    prefix = _dot((r_i > c_i).astype(BF16), onehot.astype(BF16)) + cnt_scr[...]
    gates = jnp.zeros(logits.shape, F32)
    experts = jnp.zeros(logits.shape, jnp.int32)
    for kk in range(TOP_K):
        gates = jnp.where(col == kk, es[kk] / den, gates)
        experts = jnp.where(col == kk, idxs[kk], experts)
        rank = jnp.sum(jnp.where(col == idxs[kk], prefix, 0.0), axis=-1, keepdims=True)
        experts = jnp.where(col == TOP_K + kk, rank.astype(jnp.int32), experts)
    gate_ref[...] = gates
    exp_ref[...] = experts
    cnt_scr[...] = cnt_scr[...] + jnp.sum(onehot, axis=0, keepdims=True)
    cnt_ref[...] = cnt_scr[...]


def _outproj(oa, ob, x2d, w_out, g2, wr, br, cnt0):
    n = x2d.shape[0]
    tm = ROW_TILE
    row = lambda w: pl.BlockSpec((tm, w), lambda i: (i, 0))
    full = lambda a: pl.BlockSpec(a.shape, lambda i: (0,) * a.ndim)
    return pl.pallas_call(
        _outproj_body,
        grid=(n // tm,),
        in_specs=[row(WIDTH), row(WIDTH), row(D_MODEL), full(w_out), full(g2), full(wr), full(br), full(cnt0)],
        out_specs=(row(D_MODEL), row(D_MODEL), row(LANES), row(LANES), full(cnt0)),
        out_shape=(jax.ShapeDtypeStruct((n, D_MODEL), F32), jax.ShapeDtypeStruct((n, D_MODEL), BF16),
                   jax.ShapeDtypeStruct((n, LANES), F32), jax.ShapeDtypeStruct((n, LANES), jnp.int32),
                   jax.ShapeDtypeStruct(cnt0.shape, F32)),
        scratch_shapes=[pltpu.VMEM(cnt0.shape, F32)],
        compiler_params=pltpu.CompilerParams(dimension_semantics=("arbitrary",),
                                             vmem_limit_bytes=VMEM_LIMIT),
        name="outproj",
    )(oa, ob, x2d, w_out, g2, wr, br, cnt0)


def _moe_body(be_ref, nv_ref, x_ref, wg_ref, bg_ref, wu_ref, bu_ref, wd_ref, bdn_ref, y_ref,
              wg_s, wu_s, wd_s):
    i = pl.program_id(0)
    new_expert = jnp.logical_or(i == 0, be_ref[i] != be_ref[jnp.maximum(i - 1, 0)])

    @pl.when(new_expert)
    def _():
        wg_s[...] = wg_ref[0].astype(BF16)
        wu_s[...] = wu_ref[0].astype(BF16)
        wd_s[...] = wd_ref[0].astype(BF16)

    @pl.when(i < nv_ref[0])
    def _():
        x = x_ref[...]
        gate = jnp.minimum(_dot(x, wg_s[...]) + bg_ref[0], SWIGLU_LIMIT)
        up = jnp.clip(_dot(x, wu_s[...]) + bu_ref[0], -SWIGLU_LIMIT, SWIGLU_LIMIT)
        glu = gate * _sigmoid(SWIGLU_ALPHA * gate)
        hmid = ((up + 1.0) * glu).astype(BF16)
        y_ref[...] = _dot(hmid, wd_s[...]) + bdn_ref[0]

    @pl.when(i >= nv_ref[0])
    def _():
        y_ref[...] = jnp.zeros_like(y_ref)


def _moe(block_e, nvalid, xb, wg, bg, wu, bu, wd, bdn):
    n_rows = xb.shape[0]
    nb = n_rows // MOE_ROWS
    d_ff = wg.shape[2]
    wspec = lambda s: pl.BlockSpec((1,) + s, lambda i, be, nv: (be[i], 0, 0))
    grid_spec = pltpu.PrefetchScalarGridSpec(
        num_scalar_prefetch=2,
        grid=(nb,),
        in_specs=[pl.BlockSpec((MOE_ROWS, D_MODEL), lambda i, be, nv: (i, 0)),
                  wspec((D_MODEL, d_ff)), wspec((1, d_ff)), wspec((D_MODEL, d_ff)), wspec((1, d_ff)),
                  wspec((d_ff, D_MODEL)), wspec((1, D_MODEL))],
        out_specs=pl.BlockSpec((MOE_ROWS, D_MODEL), lambda i, be, nv: (i, 0)),
        scratch_shapes=[pltpu.VMEM((D_MODEL, d_ff), BF16), pltpu.VMEM((D_MODEL, d_ff), BF16),
                        pltpu.VMEM((d_ff, D_MODEL), BF16)],
    )
    return pl.pallas_call(
        _moe_body,
        grid_spec=grid_spec,
        out_shape=jax.ShapeDtypeStruct((n_rows, D_MODEL), F32),
        compiler_params=pltpu.CompilerParams(dimension_semantics=("arbitrary",),
                                             vmem_limit_bytes=VMEM_LIMIT),
        name="moe_ffn",
    )(block_e, nvalid, xb, wg, bg, wu, bu, wd, bdn)


def _bucket_np(dist):
    n = np.maximum(dist, 0)
    max_exact = NUM_BUCKETS // 2
    n_f = np.maximum(n, max_exact).astype(np.float32)
    large = max_exact + (np.log(n_f / np.float32(max_exact)) / np.float32(math.log(MAX_DISTANCE / max_exact))
                         * np.float32(NUM_BUCKETS - max_exact)).astype(np.int32)
    large = np.minimum(large, NUM_BUCKETS - 1)
    return np.where(n < max_exact, n, large).astype(np.int32)


def _blockdiag_state(s):
    b = s.shape[0]
    ng = N_HEADS // GROUP
    s5 = s.reshape(b, ng, GROUP, HEAD_DIM, HEAD_DIM)
    eye = jnp.eye(GROUP, dtype=s.dtype)
    return jnp.einsum("bghde,hk->bghdke", s5, eye).reshape(b, ng, GW, GW)


def _unblock_state(sbd):
    b = sbd.shape[0]
    ng = N_HEADS // GROUP
    s6 = sbd.reshape(b, ng, GROUP, HEAD_DIM, GROUP, HEAD_DIM)
    d = jnp.diagonal(s6, axis1=2, axis2=4)
    return jnp.moveaxis(d, -1, 2).reshape(b, N_HEADS, HEAD_DIM, HEAD_DIM)


def _mixer(x, cache_k, cache_v, page_table, s0, cbuf, rel_bias, lw):
    (g1, w_cat, bd, qg, kg, cw, av, dtb, eb, eg, og) = lw
    b, t, _ = x.shape
    x2d = x.reshape(b * t, D_MODEL)
    prompt = cache_k is None
    outs = _inproj(x2d, g1, w_cat, bd, qg, kg, seq_len=t, attn_layout=prompt)
    qkva, z, bg, qb, kb, vb = outs[:6]
    r3 = lambda a: a.reshape(b, t, a.shape[-1])
    qkva3 = r3(qkva)
    c = min(DELTA_CHUNK, t)
    tt = min(ROW_TILE, t)
    o_a, s_fin = _gdn(qkva3, r3(z), r3(bg), cbuf, _blockdiag_state(s0), cw, av, dtb, eb, eg, bd, og,
                      tt=tt, c=c)
    new_conv = jnp.concatenate([cbuf, qkva3], axis=1)[:, t:] if t < CONV_WIDTH - 1 else qkva3[:, t - (CONV_WIDTH - 1):]
    qb3 = r3(qb)
    rb = rel_bias.astype(F32)

    if prompt:
        kbf, vt, km = outs[6:]
        heads_last = lambda a: jnp.transpose(a.reshape(b, N_HEADS, HEAD_DIM, t), (0, 3, 1, 2))
        k4, v4 = heads_last(kb), heads_last(vb)
        nb = t // MOBA_BLOCK
        kmp = jnp.pad(km.reshape(b, nb, WIDTH), ((0, 0), (0, LANES - nb), (0, 0)))
        ii = np.arange(MOBA_BLOCK)
        d_own = ii[None, :] - ii[:, None]
        bidx = np.stack([_bucket_np(d_own), _bucket_np(d_own + MOBA_BLOCK)])
        far_b = _bucket_np(np.arange(MOBA_BLOCK + 1, max(t, MOBA_BLOCK + 2)))
        assert (far_b == far_b[0]).all()
        bias_tab = jnp.transpose(rb[bidx], (3, 0, 1, 2))
        far = rb[int(far_b[0])]
        o_b = _moba_prompt(qb3, r3(kbf), vt, kmp, bias_tab, far).reshape(b * t, WIDTH)
    else:
        npages = page_table.shape[1]
        page = cache_k.shape[1]
        past = npages * page
        assert page == LANES and past % MOBA_BLOCK == 0 and t <= LANES
        n_past = past // MOBA_BLOCK
        pps = next(n for n in (32, 16, 8, 4, 2) if npages % n == 0)
        rows = N_HEADS * t
        pt_flat = page_table.reshape(-1).astype(jnp.int32)
        slab = lambda cch: jnp.transpose(cch, (0, 2, 3, 1)).reshape(cch.shape[0], WIDTH, page)
        kb3, vb3 = r3(kb), r3(vb)
        k4, v4 = (a.reshape(b, t, N_HEADS, HEAD_DIM) for a in (kb3, vb3))
        scores, kmt = _dec_scores(pt_flat, qb3, slab(cache_k), npages=npages, pps=pps)
        tpos = np.arange(t)
        d_past = past + tpos[:, None] - np.arange(past)[None, :]
        bias_past = jnp.transpose(rb[_bucket_np(d_past)], (2, 0, 1)).reshape(rows, past)
        d_own = np.zeros((t, LANES), np.int64)
        d_own[:, :t] = tpos[:, None] - tpos[None, :]
        bias_own = jnp.transpose(rb[_bucket_np(d_own)], (2, 0, 1)).reshape(rows, LANES)
        ex = np.zeros((LANES, past), np.float32)
        ex[np.arange(past) // MOBA_BLOCK, np.arange(past)] = 1.0
        probs, o_own = _dec_select(scores, kmt, qb3, kb3, vb3, bias_past, bias_own, jnp.asarray(ex, BF16),
                                   n_past=n_past)
        o_b = _dec_pv(pt_flat, probs, o_own, slab(cache_v), npages=npages, pps=pps, tq=t).reshape(b * t, WIDTH)
    return o_a.reshape(b * t, WIDTH), o_b, k4, v4, _unblock_state(s_fin), new_conv


def _moe_ffn(groups, counts, wg, bg, wu, bu, wd, bdn):
    xn = jnp.concatenate([g[0] for g in groups], axis=0)
    n = xn.shape[0]
    nk = n * TOP_K
    er = jnp.concatenate([g[2] for g in groups], axis=0)
    flat_e = er[:, :TOP_K].reshape(nk)
    rank = er[:, TOP_K:2 * TOP_K].reshape(nk)
    counts = counts[0, :N_EXPERTS].astype(jnp.int32)
    padded = ((counts + MOE_ROWS - 1) // MOE_ROWS) * MOE_ROWS
    pad_end = jnp.cumsum(padded)
    pad_start = pad_end - padded
    dest = pad_start[flat_e] + rank
    n_blocks = -(-nk // MOE_ROWS) + N_EXPERTS
    n_rows = n_blocks * MOE_ROWS
    flat_tok = jnp.arange(nk, dtype=jnp.int32) // TOP_K
    slot_tok = jnp.full((n_rows,), n, jnp.int32).at[dest].set(flat_tok, unique_indices=True)
    nvalid = (pad_end[-1] // MOE_ROWS).astype(jnp.int32).reshape(1)
    blk_start = jnp.arange(n_blocks, dtype=jnp.int32) * MOE_ROWS
    block_e = jnp.minimum(jnp.searchsorted(pad_end, blk_start, side="right"), N_EXPERTS - 1).astype(jnp.int32)
    last_e = block_e[jnp.maximum(nvalid[0] - 1, 0)]
    block_e = jnp.where(jnp.arange(n_blocks) < nvalid[0], block_e, last_e)
    x_ext = jnp.concatenate([xn, jnp.zeros((1, D_MODEL), xn.dtype)], axis=0)
    xb = x_ext[slot_tok]
    yb = _moe(block_e, nvalid, xb, wg, bg, wu, bu, wd, bdn)
    dest2 = dest.reshape(n, TOP_K)
    ys, row0 = [], 0
    for (_, gates, _, x1) in groups:
        ng = x1.shape[0]
        y = x1
        for kk in range(TOP_K):
            y = y + gates[:, kk:kk + 1] * yb[dest2[row0:row0 + ng, kk]]
        ys.append(y)
        row0 += ng
    return ys


def kernel(x_prompt, x_sample, cache_k, cache_v, state_delta, state_conv, page_table, rel_bias,
           norm1_g, w_in, conv_w, A_log, dt_bias, o_norm_g, q_norm_g, k_norm_g, w_out, norm2_g,
           w_router, b_router, w_gate, b_gate, w_up, b_up, w_down, b_down):
    depth = norm1_g.shape[0]
    bp, tp, _ = x_prompt.shape
    bs, ts, _ = x_sample.shape
    yp, ys = x_prompt, x_sample
    outs = [[] for _ in range(8)]
    seg = np.arange(WIDTH) // HEAD_DIM
    segw = np.arange(GW) // HEAD_DIM
    bd = jnp.asarray((segw[:, None] == segw[None, :]).astype(np.float32) / HEAD_DIM, BF16)
    eb_np = np.zeros((LANES, WIDTH), np.float32)
    eb_np[seg, np.arange(WIDTH)] = 1.0
    eg_np = np.zeros((LANES, WIDTH), np.float32)
    eg_np[N_HEADS + seg, np.arange(WIDTH)] = 1.0
    eb, eg = jnp.asarray(eb_np, BF16), jnp.asarray(eg_np, BF16)
    c1, c3 = QKV + WIDTH, QKV + WIDTH + 2 * N_HEADS
    for l in range(depth):
        wl = w_in[l]
        w_cat = jnp.concatenate([wl[:, :c1], wl[:, c3:], wl[:, c1:c3],
                                 jnp.zeros((D_MODEL, LANES - 2 * N_HEADS), wl.dtype)], axis=1).astype(BF16)
        tile8 = lambda g: jnp.tile(g.astype(F32), N_HEADS).reshape(1, WIDTH)
        av = jnp.zeros((1, LANES), F32).at[0, N_HEADS:2 * N_HEADS].set(-jnp.exp(A_log[l].astype(F32)))
        dtb = jnp.zeros((1, LANES), F32).at[0, N_HEADS:2 * N_HEADS].set(dt_bias[l].astype(F32))
        lw = (norm1_g[l].astype(F32).reshape(1, D_MODEL), w_cat, bd, tile8(q_norm_g[l]), tile8(k_norm_g[l]),
              conv_w[l].astype(F32), av, dtb, eb, eg, tile8(o_norm_g[l]))
        s0_p = jnp.zeros((bp, N_HEADS, HEAD_DIM, HEAD_DIM), F32)
        c0_p = jnp.zeros((bp, CONV_WIDTH - 1, QKV), F32)
        oa_p, ob_p, k_p, v_p, s_p, c_p = _mixer(yp, None, None, None, s0_p, c0_p, rel_bias, lw)
        oa_s, ob_s, k_s, v_s, s_s, c_s = _mixer(ys, cache_k[l], cache_v[l], page_table, state_delta[l],
                                                state_conv[l], rel_bias, lw)
        w_o = w_out[l].astype(BF16)
        g2 = norm2_g[l].astype(F32).reshape(1, D_MODEL)
        wr = jnp.pad(w_router[l].astype(F32), ((0, 0), (0, LANES - N_EXPERTS)))
        br = jnp.pad(b_router[l].astype(F32), (0, LANES - N_EXPERTS)).reshape(1, LANES)
        cnt0 = jnp.zeros((1, LANES), F32)
        x1_p, xn_p, gt_p, er_p, cnt_p = _outproj(oa_p, ob_p, yp.reshape(bp * tp, D_MODEL), w_o, g2, wr, br, cnt0)
        x1_s, xn_s, gt_s, er_s, cnt = _outproj(oa_s, ob_s, ys.reshape(bs * ts, D_MODEL), w_o, g2, wr, br, cnt_p)
        y_p, y_s = _moe_ffn([(xn_p, gt_p, er_p, x1_p), (xn_s, gt_s, er_s, x1_s)], cnt,
                            w_gate[l].astype(F32), b_gate[l].astype(F32)[:, None, :],
                            w_up[l].astype(F32), b_up[l].astype(F32)[:, None, :],
                            w_down[l].astype(F32), b_down[l].astype(F32)[:, None, :])
        yp = y_p.reshape(bp, tp, D_MODEL)
        ys = y_s.reshape(bs, ts, D_MODEL)
        for lst, val in zip(outs, (k_p, v_p, s_p, c_p, k_s, v_s, s_s, c_s)):
            lst.append(val)
    stacked = [jnp.stack(o) for o in outs]
    return (yp, ys, *stacked)
```

```python
import functools
import math

import numpy as np
import jax
import jax.numpy as jnp
from jax import lax
from jax.experimental import pallas as pl
from jax.experimental.pallas import tpu as pltpu

F32 = jnp.float32
BF16 = jnp.bfloat16

D_MODEL = 1024
N_HEADS = 8
HEAD_DIM = 64
WIDTH = N_HEADS * HEAD_DIM
QKV = 3 * WIDTH
CONV_WIDTH = 4
DELTA_CHUNK = 64
MOBA_BLOCK = 256
MOBA_TOP = 3
ATTN_SCALE = HEAD_DIM ** -0.5
NUM_BUCKETS = 32
MAX_DISTANCE = 128
N_EXPERTS = 32
TOP_K = 4
SWIGLU_LIMIT = 7.0
SWIGLU_ALPHA = 1.702
RMS_EPS = 1e-6

LANES = 128
SUBLANES = 8
ROW_TILE = 256
MOE_ROWS = 512
NEG = -1e30
VMEM_LIMIT = 48 * 1024 * 1024

GROUP = 4
GW = GROUP * HEAD_DIM
NPAIR = WIDTH // LANES
FAR_BLOCKS = 4

_NN = (((1,), (0,)), ((), ()))
_NT = (((1,), (1,)), ((), ()))
_TN = (((0,), (0,)), ((), ()))


def _dot(a, b, dims=_NN):
    return lax.dot_general(a, b, dims, preferred_element_type=F32)


def _split2(a):
    hi = a.astype(BF16)
    lo = (a - hi.astype(F32)).astype(BF16)
    return hi, lo


def _split3(a):
    hi = a.astype(BF16)
    r = a - hi.astype(F32)
    mid = r.astype(BF16)
    lo = (r - mid.astype(F32)).astype(BF16)
    return hi, mid, lo


def _mm1(a, b, dims=_NN):
    return _dot(a.astype(BF16), b.astype(BF16), dims)


def _mm3(a, b, dims=_NN):
    ah, al = _split2(a)
    bh, bl = _split2(b)
    return _dot(ah, bh, dims) + (_dot(ah, bl, dims) + _dot(al, bh, dims))


def _mm_exact_rhs(a, b_bf16, dims=_NN):
    hi, mid, lo = _split3(a)
    return _dot(hi, b_bf16, dims) + (_dot(mid, b_bf16, dims) + _dot(lo, b_bf16, dims))


def _mm_exact_lhs(a_bf16, b, dims=_NN):
    hi, mid, lo = _split3(b)
    return _dot(a_bf16, hi, dims) + (_dot(a_bf16, mid, dims) + _dot(a_bf16, lo, dims))


def _seg_mean(x, bd):
    outs = []
    for g in range(x.shape[1] // GW):
        hi, lo = _split2(x[:, g * GW:(g + 1) * GW])
        outs.append(_dot(hi, bd) + _dot(lo, bd))
    return outs[0] if len(outs) == 1 else jnp.concatenate(outs, axis=1)


def _sigmoid(x):
    return 1.0 / (1.0 + jnp.exp(-x))


def _softplus(x):
    return jnp.maximum(x, 0.0) + jnp.log(1.0 + jnp.exp(-jnp.abs(x)))


def _inproj_body(x_ref, g_ref, w_ref, bd_ref, qg_ref, kg_ref,
                 qkva_ref, z_ref, bg_ref, qb_ref, kb_ref, vb_ref, *attn_refs):
    x = x_ref[...]
    ms = jnp.mean(x * x, axis=-1, keepdims=True)
    h = (x * lax.rsqrt(ms + RMS_EPS) * g_ref[...]).astype(BF16)

    def proj(lo, hi):
        return _dot(h, w_ref[:, lo:hi])

    bd = bd_ref[...]
    qkva_ref[...] = proj(0, QKV)
    z_ref[...] = proj(QKV, QKV + WIDTH)
    c = QKV + WIDTH
    q = proj(c, c + WIDTH)
    qb_ref[...] = q * lax.rsqrt(_seg_mean(q * q, bd) + RMS_EPS) * qg_ref[...]
    k = proj(c + WIDTH, c + 2 * WIDTH)
    kn = k * lax.rsqrt(_seg_mean(k * k, bd) + RMS_EPS) * kg_ref[...]
    v = proj(c + 2 * WIDTH, c + 3 * WIDTH)
    bg_ref[...] = proj(c + 3 * WIDTH, c + 3 * WIDTH + LANES)
    if attn_refs:
        kbf_ref, vt_ref, km_ref = attn_refs
        kb_ref[0] = kn.T
        vt = v.T
        vb_ref[0] = vt
        kbf_ref[...] = kn.astype(BF16)
        km_ref[0] = jnp.mean(kn, axis=0, keepdims=True)
        vtb = vt.astype(BF16)
        for pp in range(NPAIR):
            vt_ref[0, pp, 0] = vtb[pp * LANES:(pp + 1) * LANES, :]
    else:
        kb_ref[...] = kn
        vb_ref[...] = v


def _inproj(x2d, g1, w_cat, bd, qg, kg, *, seq_len, attn_layout):
    n = x2d.shape[0]
    tm = ROW_TILE
    nt = n // tm
    row = lambda w: pl.BlockSpec((tm, w), lambda i: (i, 0))
    full = lambda a: pl.BlockSpec(a.shape, lambda i: (0,) * a.ndim)
    out_shape = [
        jax.ShapeDtypeStruct((n, QKV), F32), jax.ShapeDtypeStruct((n, WIDTH), F32),
        jax.ShapeDtypeStruct((n, LANES), F32), jax.ShapeDtypeStruct((n, WIDTH), F32),
    ]
    out_specs = [row(QKV), row(WIDTH), row(LANES), row(WIDTH)]
    if not attn_layout:
        out_shape += [jax.ShapeDtypeStruct((n, WIDTH), F32)] * 2
        out_specs += [row(WIDTH)] * 2
    else:
        tpb = seq_len // tm
        kv_t = jax.ShapeDtypeStruct((n // seq_len, WIDTH, seq_len), F32)
        kv_spec = pl.BlockSpec((1, WIDTH, tm), lambda i: (i // tpb, 0, i % tpb))
        out_shape += [kv_t, kv_t]
        out_specs += [kv_spec, kv_spec]
        out_shape += [jax.ShapeDtypeStruct((n, WIDTH), BF16),
                      jax.ShapeDtypeStruct((n // seq_len, NPAIR, tpb, LANES, tm), BF16),
                      jax.ShapeDtypeStruct((nt, 1, WIDTH), F32)]
        out_specs += [row(WIDTH),
                      pl.BlockSpec((1, NPAIR, 1, LANES, tm), lambda i: (i // tpb, 0, i % tpb, 0, 0)),
                      pl.BlockSpec((1, 1, WIDTH), lambda i: (i, 0, 0))]
    return pl.pallas_call(
        _inproj_body,
        grid=(nt,),
        in_specs=[row(D_MODEL), full(g1), full(w_cat), full(bd), full(qg), full(kg)],
        out_specs=tuple(out_specs),
        out_shape=tuple(out_shape),
        compiler_params=pltpu.CompilerParams(dimension_semantics=("parallel",),
                                             vmem_limit_bytes=VMEM_LIMIT),
        name="inproj",
    )(x2d, g1, w_cat, bd, qg, kg)


def _stack(x, masks):
    return jnp.concatenate([jnp.where(m, x, 0.0) for m in masks], axis=0)


def _unstack(w, c, n):
    acc = w[0:c]
    for h in range(1, n):
        acc = acc + w[h * c:(h + 1) * c]
    return acc


def _gdn_body(qkva_ref, z_ref, bg_ref, cbuf_ref, s0_ref, cw_ref, av_ref, dtb_ref, eb_ref, eg_ref,
              bd_ref, og_ref, o_ref, sfin_ref, s_scr, carry_scr, uext_scr, *, tt, c):
    t = pl.program_id(1)
    nt = pl.num_programs(1)
    gc = GROUP * c

    @pl.when(t == 0)
    def _():
        s_scr[...] = s0_ref[0]
        carry_scr[...] = jnp.zeros_like(carry_scr)
        carry_scr[SUBLANES - (CONV_WIDTH - 1):SUBLANES, :] = cbuf_ref[0]

    uext_scr[0:SUBLANES, :] = carry_scr[...]
    uext_scr[SUBLANES:SUBLANES + tt, :] = qkva_ref[0]
    carry_scr[...] = uext_scr[tt:tt + SUBLANES, :]

    cw = cw_ref[...]
    off = SUBLANES - (CONV_WIDTH - 1)
    conv = cw[0:1] * uext_scr[off:off + tt, :]
    for i in range(1, CONV_WIDTH):
        conv = conv + cw[i:i + 1] * uext_scr[off + i:off + i + tt, :]
    act = conv * _sigmoid(conv)

    bd = bd_ref[...]
    q = act[:, 0:WIDTH]
    k = act[:, WIDTH:2 * WIDTH]
    v = act[:, 2 * WIDTH:3 * WIDTH]
    q = q * lax.rsqrt(_seg_mean(q * q, bd) * HEAD_DIM + RMS_EPS) * (HEAD_DIM ** -0.5)
    k = k * lax.rsqrt(_seg_mean(k * k, bd) * HEAD_DIM + RMS_EPS)

    bg = bg_ref[0]
    beta = _mm_exact_rhs(_sigmoid(bg), eb_ref[...])
    gx = _mm_exact_rhs(av_ref[...] * _softplus(bg + dtb_ref[...]), eg_ref[...])

    r4 = lax.broadcasted_iota(jnp.int32, (gc, gc), 0)
    c4 = lax.broadcasted_iota(jnp.int32, (gc, gc), 1)
    same = (r4 // c) == (c4 // c)
    tri = jnp.logical_and(same, r4 >= c4)
    strict = jnp.logical_and(same, r4 > c4)
    eye = (r4 == c4).astype(F32)
    rc = lax.broadcasted_iota(jnp.int32, (c, c), 0)
    cc = lax.broadcasted_iota(jnp.int32, (c, c), 1)
    tril_c = (rc >= cc).astype(BF16)
    lane_g = lax.broadcasted_iota(jnp.int32, (1, GW), 1) // HEAD_DIM
    hmasks = [lane_g == h for h in range(GROUP)]
    rs = lax.broadcasted_iota(jnp.int32, (GW, GW), 0) // HEAD_DIM
    cs = lax.broadcasted_iota(jnp.int32, (GW, GW), 1) // HEAD_DIM
    bdm = (rs == cs).astype(F32)
    nsq = int(round(math.log2(c))) - 1

    nch, ngr = tt // c, N_HEADS // GROUP
    inst = []
    for ch in range(nch):
        r0 = ch * c
        gcum = _mm_exact_lhs(tril_c, gx[r0:r0 + c])
        eg = jnp.exp(gcum)
        glast = gcum[c - 1:c, :]
        kscale = jnp.exp(glast - gcum)
        gtot = jnp.exp(glast)
        for gr in range(ngr):
            sl = slice(gr * GW, (gr + 1) * GW)
            kq = k[r0:r0 + c, sl]
            qq = q[r0:r0 + c, sl]
            bb = beta[r0:r0 + c, sl]
            gq = gcum[:, sl]
            kb = kq * bb
            xk = _stack(kq, hmasks).astype(BF16)
            kk = _dot(_stack(kb, hmasks).astype(BF16), xk, _NT)
            qk = _dot(_stack(qq, hmasks).astype(BF16), xk, _NT)
            gcol = jnp.concatenate(
                [jnp.broadcast_to(gq[:, h * HEAD_DIM:h * HEAD_DIM + 1], (c, gc)) for h in range(GROUP)], axis=0)
            if gc % LANES == 0:
                grow = gcol.T
            else:
                grow = _mm_exact_lhs(jnp.full((gc, GW), 1.0 / HEAD_DIM, BF16), _stack(gq, hmasks), _NT)
            dm = jnp.where(tri, jnp.exp(jnp.where(tri, gcol - grow, 0.0)), 0.0)
            lmat = jnp.where(strict, kk * dm, 0.0)
            rhs = jnp.concatenate([_stack(v[r0:r0 + c, sl] * bb, hmasks), _stack(kb * eg[:, sl], hmasks)], axis=1)
            inst.append(dict(gr=gr, sl=sl, amat=qk * dm, pinv=eye - lmat, msq=lmat, rhs=rhs,
                             qe=qq * eg[:, sl], kd=kq * kscale[:, sl], gtot=gtot[:, sl]))
    for _ in range(nsq):
        for it in inst:
            it["msq"] = _mm1(it["msq"], it["msq"])
        for it in inst:
            it["pinv"] = it["pinv"] + _mm1(it["pinv"], it["msq"])
    for it in inst:
        w = _mm1(it["pinv"], it["rhs"])
        it["value"] = _unstack(w[:, 0:GW], c, GROUP)
        it["kcum"] = _unstack(w[:, GW:2 * GW], c, GROUP)
    o_rows = []
    for ch in range(nch):
        o_groups = []
        for it in inst[ch * ngr:(ch + 1) * ngr]:
            s = s_scr[it["gr"]]
            u = it["value"] - _mm3(it["kcum"], s)
            o_groups.append(_mm1(it["qe"], s) + _unstack(_mm1(it["amat"], _stack(u, hmasks)), c, GROUP))
            s_scr[it["gr"]] = s * it["gtot"] + bdm * _mm3(it["kd"], u, _TN)
        o_rows.append(jnp.concatenate(o_groups, axis=1))
    o = o_rows[0] if len(o_rows) == 1 else jnp.concatenate(o_rows, axis=0)
    on = o * lax.rsqrt(_seg_mean(o * o, bd) + RMS_EPS) * og_ref[...]
    z = z_ref[0]
    o_ref[0] = on * (z * _sigmoid(z))

    @pl.when(t == nt - 1)
    def _():
        sfin_ref[0] = s_scr[...]


def _gdn(qkva, z, bg, cbuf, s0bd, cw, av, dtb, eb, eg, bd, og, *, tt, c):
    b, t, _ = qkva.shape
    ng = N_HEADS // GROUP
    full = lambda a: pl.BlockSpec(a.shape, lambda i, j: (0,) * a.ndim)
    body = functools.partial(_gdn_body, tt=tt, c=c)
    return pl.pallas_call(
        body,
        grid=(b, t // tt),
        in_specs=[
            pl.BlockSpec((1, tt, QKV), lambda i, j: (i, j, 0)),
            pl.BlockSpec((1, tt, WIDTH), lambda i, j: (i, j, 0)),
            pl.BlockSpec((1, tt, LANES), lambda i, j: (i, j, 0)),
            pl.BlockSpec((1, CONV_WIDTH - 1, QKV), lambda i, j: (i, 0, 0)),
            pl.BlockSpec((1, ng, GW, GW), lambda i, j: (i, 0, 0, 0)),
            full(cw), full(av), full(dtb), full(eb), full(eg), full(bd), full(og),
        ],
        out_specs=(pl.BlockSpec((1, tt, WIDTH), lambda i, j: (i, j, 0)),
                   pl.BlockSpec((1, ng, GW, GW), lambda i, j: (i, 0, 0, 0))),
        out_shape=(jax.ShapeDtypeStruct((b, t, WIDTH), F32),
                   jax.ShapeDtypeStruct((b, ng, GW, GW), F32)),
        scratch_shapes=[pltpu.VMEM((ng, GW, GW), F32), pltpu.VMEM((SUBLANES, QKV), F32),
                        pltpu.VMEM((tt + SUBLANES, QKV), F32)],
        compiler_params=pltpu.CompilerParams(dimension_semantics=("arbitrary", "arbitrary"),
                                             vmem_limit_bytes=VMEM_LIMIT),
        name="gdn",
    )(qkva, z, bg, cbuf, s0bd, cw, av, dtb, eb, eg, bd, og)


def _bucket_np(dist):
    n = np.maximum(dist, 0)
    max_exact = NUM_BUCKETS // 2
    n_f = np.maximum(n, max_exact).astype(np.float32)
    large = max_exact + (np.log(n_f / np.float32(max_exact)) / np.float32(math.log(MAX_DISTANCE / max_exact))
                         * np.float32(NUM_BUCKETS - max_exact)).astype(np.int32)
    large = np.minimum(large, NUM_BUCKETS - 1)
    return np.where(n < max_exact, n, large).astype(np.int32)


def _bucket_steps(max_dist):
    b = _bucket_np(np.arange(max_dist + 1))
    assert (np.diff(b) >= 0).all()
    firsts = np.flatnonzero(np.diff(b, prepend=-1))
    return [(int(d), int(b[d])) for d in firsts]


def _bias_body(rb_ref, d_ref, o_ref, *, steps):
    h = pl.program_id(0)
    d = d_ref[...]
    acc = jnp.full(d.shape, rb_ref[steps[0][1] * N_HEADS + h], F32)
    for first, bucket in steps[1:]:
        acc = jnp.where(d >= first, rb_ref[bucket * N_HEADS + h], acc)
    o_ref[0] = acc


def _bias_table(rel_bias, dist_np):
    r, c = dist_np.shape
    steps = _bucket_steps(int(max(dist_np.max(), 0)))
    grid_spec = pltpu.PrefetchScalarGridSpec(
        num_scalar_prefetch=1,
        grid=(N_HEADS,),
        in_specs=[pl.BlockSpec((r, c), lambda h, rb: (0, 0))],
        out_specs=pl.BlockSpec((1, r, c), lambda h, rb: (h, 0, 0)),
    )
    return pl.pallas_call(
        functools.partial(_bias_body, steps=steps),
        grid_spec=grid_spec,
        out_shape=jax.ShapeDtypeStruct((N_HEADS, r, c), F32),
        compiler_params=pltpu.CompilerParams(dimension_semantics=("arbitrary",),
                                             vmem_limit_bytes=VMEM_LIMIT),
        name="bias_table",
    )(rel_bias.astype(F32).reshape(-1), jnp.asarray(np.maximum(dist_np, 0), jnp.int32))


def _top_select(gate, idx, n_cand, n_valid_f, axis):
    g = jnp.where(idx < n_cand, gate, -jnp.inf)
    sel = jnp.zeros(gate.shape, F32)
    for r in range(MOBA_TOP):
        mx = jnp.max(g, axis=axis, keepdims=True)
        first = jnp.min(jnp.where(g == mx, idx, jnp.int32(1 << 30)), axis=axis, keepdims=True)
        hit = idx == first
        sel = jnp.maximum(sel, jnp.where(hit, n_valid_f[r], 0.0))
        g = jnp.where(hit, -jnp.inf, g)
    return sel


def _moba_prompt_body(far_ref, q_ref, k_ref, vt_ref, km_ref, bias_ref, o_ref, sel_scr):
    p = pl.program_id(1)
    i = pl.program_id(2)
    blk = MOBA_BLOCK
    qt = (q_ref[0] * ATTN_SCALE).T
    row = lax.broadcasted_iota(jnp.int32, (LANES, blk), 0)
    top = row < HEAD_DIM
    qt_m = [jnp.where(top, qt, 0.0), jnp.where(top, 0.0, qt)]
    qtb = [x.astype(BF16) for x in qt_m]
    km = km_ref[0]
    n_valid = [(i > r).astype(F32) for r in range(MOBA_TOP)]
    for hh in range(2):
        sel_scr[hh] = _top_select(_mm3(km, qt_m[hh]), row, i, n_valid, 0)
    rowk = lax.broadcasted_iota(jnp.int32, (blk, blk), 0)
    colq = lax.broadcasted_iota(jnp.int32, (blk, blk), 1)

    def scores(n, nblk):
        kblks = [k_ref[0, pl.ds(pl.multiple_of((n + j) * blk, blk), blk), :] for j in range(nblk)]
        return [[_dot(kb_, qtb[hh]) for kb_ in kblks] for hh in range(2)]

    def update(n, carry, modes, raw):
        m0, l0, m1, l1, acc = carry
        ms, ls = [m0, m1], [l0, l1]
        vts = [vt_ref[0, 0, n + j] for j in range(len(modes))]
        alphas, pvs = [], []
        for hh in range(2):
            ss, shifts, valids, bmax = [], [], [], []
            for j, mode in enumerate(modes):
                s = raw[hh][j]
                if mode == "own":
                    s = jnp.where(rowk <= colq, s + bias_ref[hh, 0], NEG)
                    shift, valid = 0.0, None
                else:
                    if mode == "prev":
                        s, shift = s + bias_ref[hh, 1], 0.0
                    else:
                        shift = far_ref[2 * p + hh]
                    valid = sel_scr[hh, pl.ds(n + j, 1), :] > 0.5
                bm = jnp.max(s, axis=0, keepdims=True) + shift
                ss.append(s)
                shifts.append(shift)
                valids.append(valid)
                bmax.append(bm if valid is None else jnp.where(valid, bm, NEG))
            m_new = ms[hh]
            for bm in bmax:
                m_new = jnp.maximum(m_new, bm)
            alpha = jnp.exp(ms[hh] - m_new)
            lsum = alpha * ls[hh]
            pv = None
            for j, s in enumerate(ss):
                off = m_new - shifts[j]
                if valids[j] is not None:
                    off = jnp.where(valids[j], off, -NEG)
                pr = jnp.exp(s - off)
                lsum = lsum + jnp.sum(pr, axis=0, keepdims=True)
                d = _dot(vts[j], pr.astype(BF16))
                pv = d if pv is None else pv + d
            ls[hh] = lsum
            ms[hh] = m_new
            alphas.append(alpha)
            pvs.append(pv)
        acc = acc * jnp.where(top, alphas[0], alphas[1]) + jnp.where(top, pvs[0], pvs[1])
        return ms[0], ls[0], ms[1], ls[1], acc

    init = (jnp.full((1, blk), NEG, F32), jnp.zeros((1, blk), F32),
            jnp.full((1, blk), NEG, F32), jnp.zeros((1, blk), F32),
            jnp.zeros((LANES, blk), F32))
    def step(n, carry, modes):
        return update(n, carry, modes, scores(n, len(modes)))

    n_far = jnp.maximum(i - 1, 0)
    n_wide = n_far // FAR_BLOCKS
    wide = ("far",) * FAR_BLOCKS
    n_blocks = k_ref.shape[1] // blk
    last_start = max(n_blocks - FAR_BLOCKS, 0)

    def wide_step(w, cr):
        soft, raw = cr
        nxt = scores(jnp.clip((w + 1) * FAR_BLOCKS, 0, last_start), FAR_BLOCKS)
        return update(w * FAR_BLOCKS, soft, wide, raw), nxt

    if n_blocks >= FAR_BLOCKS:
        carry, _ = lax.fori_loop(0, n_wide, wide_step, (init, scores(0, FAR_BLOCKS)))
    else:
        carry = init
    done = n_wide * FAR_BLOCKS
    width = FAR_BLOCKS // 2
    while width >= 1:
        take = ((n_far - done) >= width).astype(jnp.int32)
        carry = lax.fori_loop(0, take, lambda _, cr, d=done, w=width: step(d, cr, ("far",) * w), carry)
        done = done + take * width
        width //= 2
    has_prev = (i >= 1).astype(jnp.int32)
    carry = lax.fori_loop(0, has_prev, lambda _, cr: step(i - 1, cr, ("prev", "own")), carry)
    m0, l0, m1, l1, acc = lax.fori_loop(0, 1 - has_prev, lambda _, cr: step(i, cr, ("own",)), carry)
    o_ref[0] = (acc / jnp.where(top, l0, l1)).T


def _moba_prompt(qb, kbf, vt, kmp, bias_tab, far):
    b, t, _ = qb.shape
    nq = t // MOBA_BLOCK
    grid_spec = pltpu.PrefetchScalarGridSpec(
        num_scalar_prefetch=1,
        grid=(b, NPAIR, nq),
        in_specs=[
            pl.BlockSpec((1, MOBA_BLOCK, LANES), lambda bi, p, i, far: (bi, i, p)),
            pl.BlockSpec((1, t, LANES), lambda bi, p, i, far: (bi, 0, p)),
            pl.BlockSpec((1, 1, nq, LANES, MOBA_BLOCK), lambda bi, p, i, far: (bi, p, 0, 0, 0)),
            pl.BlockSpec((1, LANES, LANES), lambda bi, p, i, far: (bi, 0, p)),
            pl.BlockSpec((2, 2, MOBA_BLOCK, MOBA_BLOCK), lambda bi, p, i, far: (p, 0, 0, 0)),
        ],
        out_specs=pl.BlockSpec((1, MOBA_BLOCK, LANES), lambda bi, p, i, far: (bi, i, p)),
        scratch_shapes=[pltpu.VMEM((2, LANES, MOBA_BLOCK), F32)],
    )
    return pl.pallas_call(
        _moba_prompt_body,
        grid_spec=grid_spec,
        out_shape=jax.ShapeDtypeStruct((b, t, WIDTH), F32),
        compiler_params=pltpu.CompilerParams(
            dimension_semantics=("arbitrary", "arbitrary", "arbitrary"),
            vmem_limit_bytes=VMEM_LIMIT),
        name="moba_prompt",
    )(far, qb, kbf, vt, kmp, bias_tab)


def _head_masks(width):
    lane_h = lax.broadcasted_iota(jnp.int32, (1, width), 1) // HEAD_DIM
    return [lane_h == h for h in range(width // HEAD_DIM)]


def _dec_scores_body(pt_ref, q_ref, *refs, pps):
    k_refs = refs[:pps]
    s_ref, kmt_ref = refs[pps], refs[pps + 1]
    masks = _head_masks(WIDTH)
    qs = _stack(q_ref[0] * ATTN_SCALE, masks).astype(BF16)
    ppb = MOBA_BLOCK // LANES
    j = pl.program_id(1)
    for i in range(pps):
        s_ref[0, :, i * LANES:(i + 1) * LANES] = _dot(qs, k_refs[i][0].astype(BF16))
    lane = lax.broadcasted_iota(jnp.int32, (1, LANES), 1)
    @pl.when(j == 0)
    def _():
        kmt_ref[0] = jnp.zeros((WIDTH, LANES), F32)

    kmt = kmt_ref[0]
    for blk in range(pps // ppb):
        tot = k_refs[blk * ppb][0]
        for jj in range(1, ppb):
            tot = tot + k_refs[blk * ppb + jj][0]
        mean = jnp.sum(tot, axis=1, keepdims=True) * (1.0 / MOBA_BLOCK)
        kmt = jnp.where(lane == j * (pps // ppb) + blk, mean, kmt)
    kmt_ref[0] = kmt


def _page_specs(pps, npages):
    def mk(i):
        return pl.BlockSpec((1, WIDTH, LANES), lambda b, j, pt: (pt[b * npages + j * pps + i], 0, 0))
    return [mk(i) for i in range(pps)]


def _dec_scores(pt_flat, q, cache, *, npages, pps):
    b, tq, _ = q.shape
    ppb = MOBA_BLOCK // LANES
    nj = npages // pps
    grid_spec = pltpu.PrefetchScalarGridSpec(
        num_scalar_prefetch=1,
        grid=(b, nj),
        in_specs=[pl.BlockSpec((1, tq, WIDTH), lambda bi, j, pt: (bi, 0, 0))] + _page_specs(pps, npages),
        out_specs=(pl.BlockSpec((1, N_HEADS * tq, pps * LANES), lambda bi, j, pt: (bi, 0, j)),
                   pl.BlockSpec((1, WIDTH, LANES), lambda bi, j, pt: (bi, 0, 0))),
    )
    assert npages // ppb <= LANES
    return pl.pallas_call(
        functools.partial(_dec_scores_body, pps=pps),
        grid_spec=grid_spec,
        out_shape=(jax.ShapeDtypeStruct((b, N_HEADS * tq, npages * LANES), F32),
                   jax.ShapeDtypeStruct((b, WIDTH, LANES), F32)),
        compiler_params=pltpu.CompilerParams(dimension_semantics=("arbitrary", "arbitrary"),
                                             vmem_limit_bytes=VMEM_LIMIT),
        name="dec_scores",
    )(pt_flat, q, *([cache] * pps))


def _dec_select_body(s_ref, km_ref, q_ref, kn_ref, vn_ref, bp_ref, bo_ref, ex_ref, p_ref, oo_ref,
                     *, tq, n_past):
    rows = N_HEADS * tq
    masks = _head_masks(WIDTH)
    qs = _stack(q_ref[0] * ATTN_SCALE, masks)
    col = lax.broadcasted_iota(jnp.int32, (rows, LANES), 1)
    n_valid = [jnp.float32(1.0 if n_past > r else 0.0) for r in range(MOBA_TOP)]
    sel = _top_select(_mm3(qs, km_ref[0]), col, n_past, n_valid, -1)
    selk = _dot(sel.astype(BF16), ex_ref[...])
    logit = jnp.where(selk > 0.5, s_ref[0] + bp_ref[...], NEG)
    pad = jnp.zeros((LANES - tq, WIDTH), F32)
    kn = jnp.concatenate([kn_ref[0], pad], axis=0).astype(BF16)
    vn = jnp.concatenate([vn_ref[0], pad], axis=0).astype(BF16)
    trow = lax.rem(lax.broadcasted_iota(jnp.int32, (rows, LANES), 0), tq)
    s_own = jnp.where(col <= trow, _dot(qs.astype(BF16), kn, _NT) + bo_ref[...], NEG)
    m = jnp.maximum(jnp.max(logit, axis=-1, keepdims=True), jnp.max(s_own, axis=-1, keepdims=True))
    pr = jnp.exp(logit - m)
    po = jnp.exp(s_own - m)
    inv = 1.0 / (jnp.sum(pr, axis=-1, keepdims=True) + jnp.sum(po, axis=-1, keepdims=True))
    p_ref[0] = (pr * inv).astype(BF16)
    oo_ref[0] = _dot((po * inv).astype(BF16), vn)


def _dec_select(scores, kmp, q, kn, vn, bias_past, bias_own, expand, *, n_past):
    b, tq, _ = q.shape
    rows = N_HEADS * tq
    plen = scores.shape[2]
    full = lambda a: pl.BlockSpec(a.shape, lambda i: (0,) * a.ndim)
    per = lambda a: pl.BlockSpec((1,) + a.shape[1:], lambda i: (i,) + (0,) * (a.ndim - 1))
    return pl.pallas_call(
        functools.partial(_dec_select_body, tq=tq, n_past=n_past),
        grid=(b,),
        in_specs=[per(scores), per(kmp), per(q), per(kn), per(vn), full(bias_past), full(bias_own),
                  full(expand)],
        out_specs=(pl.BlockSpec((1, rows, plen), lambda i: (i, 0, 0)),
                   pl.BlockSpec((1, rows, WIDTH), lambda i: (i, 0, 0))),
        out_shape=(jax.ShapeDtypeStruct((b, rows, plen), BF16),
                   jax.ShapeDtypeStruct((b, rows, WIDTH), F32)),
        compiler_params=pltpu.CompilerParams(dimension_semantics=("arbitrary",),
                                             vmem_limit_bytes=VMEM_LIMIT),
        name="dec_select",
    )(scores, kmp, q, kn, vn, bias_past, bias_own, expand)


def _dec_pv_body(pt_ref, p_ref, oo_ref, *refs, pps, tq):
    v_refs = refs[:pps]
    o_ref, acc = refs[pps], refs[pps + 1]
    j = pl.program_id(1)

    @pl.when(j == 0)
    def _():
        acc[...] = oo_ref[0]

    tot = acc[...]
    for i in range(pps):
        tot = tot + _dot(p_ref[0, :, i * LANES:(i + 1) * LANES], v_refs[i][0].astype(BF16), _NT)
    acc[...] = tot

    @pl.when(j == pl.num_programs(1) - 1)
    def _():
        masks = _head_masks(WIDTH)
        a = acc[...]
        out = jnp.where(masks[0], a[0:tq], 0.0)
        for h in range(1, N_HEADS):
            out = out + jnp.where(masks[h], a[h * tq:(h + 1) * tq], 0.0)
        o_ref[0] = out


def _dec_pv(pt_flat, probs, o_own, cache, *, npages, pps, tq):
    b, rows, _ = probs.shape
    grid_spec = pltpu.PrefetchScalarGridSpec(
        num_scalar_prefetch=1,
        grid=(b, npages // pps),
        in_specs=[pl.BlockSpec((1, rows, pps * LANES), lambda bi, j, pt: (bi, 0, j)),
                  pl.BlockSpec((1, rows, WIDTH), lambda bi, j, pt: (bi, 0, 0))] + _page_specs(pps, npages),
        out_specs=pl.BlockSpec((1, tq, WIDTH), lambda bi, j, pt: (bi, 0, 0)),
        scratch_shapes=[pltpu.VMEM((rows, WIDTH), F32)],
    )
    return pl.pallas_call(
        functools.partial(_dec_pv_body, pps=pps, tq=tq),
        grid_spec=grid_spec,
        out_shape=jax.ShapeDtypeStruct((b, tq, WIDTH), F32),
        compiler_params=pltpu.CompilerParams(dimension_semantics=("arbitrary", "arbitrary"),
                                             vmem_limit_bytes=VMEM_LIMIT),
        name="dec_pv",
    )(pt_flat, probs, o_own, *([cache] * pps))


def _outproj_body(oa_ref, ob_ref, x_ref, w_ref, g_ref, wr_ref, br_ref, cnt0_ref,
                  x1_ref, xn_ref, gate_ref, exp_ref, cnt_ref, cnt_scr):
    @pl.when(pl.program_id(0) == 0)
    def _():
        cnt_scr[...] = cnt0_ref[...]

    mixed = _dot(oa_ref[...].astype(BF16), w_ref[0:WIDTH, :]) + _dot(ob_ref[...].astype(BF16),
                                                                      w_ref[WIDTH:2 * WIDTH, :])
    x1 = x_ref[...] + mixed
    x1_ref[...] = x1
    ms = jnp.mean(x1 * x1, axis=-1, keepdims=True)
    xn = x1 * lax.rsqrt(ms + RMS_EPS) * g_ref[...]
    xn_ref[...] = xn.astype(BF16)
    logits = _mm3(xn, wr_ref[...]) + br_ref[...]
    col = lax.broadcasted_iota(jnp.int32, logits.shape, 1)
    g = jnp.where(col < N_EXPERTS, logits, -jnp.inf)
    vals, idxs = [], []
    for _ in range(TOP_K):
        mx = jnp.max(g, axis=-1, keepdims=True)
        idx = jnp.min(jnp.where(g == mx, col, jnp.int32(1 << 30)), axis=-1, keepdims=True)
        vals.append(mx)
        idxs.append(idx)
        g = jnp.where(col == idx, -jnp.inf, g)
    es = [jnp.exp(vv - vals[0]) for vv in vals]
    den = es[0]
    for e in es[1:]:
        den = den + e
    tm = logits.shape[0]
    onehot = jnp.zeros(logits.shape, F32)
    for kk in range(TOP_K):
        onehot = onehot + (col == idxs[kk]).astype(F32)
    r_i = lax.broadcasted_iota(jnp.int32, (tm, tm), 0)
    c_i = lax.broadcasted_iota(jnp.int32, (tm, tm), 1)
    prefix = _dot((r_i > c_i).astype(BF16), onehot.astype(BF16)) + cnt_scr[...]
    gates = jnp.zeros(logits.shape, F32)
    experts = jnp.zeros(logits.shape, jnp.int32)
    for kk in range(TOP_K):
        gates = jnp.where(col == kk, es[kk] / den, gates)
        experts = jnp.where(col == kk, idxs[kk], experts)
        rank = jnp.sum(jnp.where(col == idxs[kk], prefix, 0.0), axis=-1, keepdims=True)
        experts = jnp.where(col == TOP_K + kk, rank.astype(jnp.int32), experts)
    gate_ref[...] = gates
    exp_ref[...] = experts
    cnt_scr[...] = cnt_scr[...] + jnp.sum(onehot, axis=0, keepdims=True)
    cnt_ref[...] = cnt_scr[...]


def _outproj(oa, ob, x2d, w_out, g2, wr, br, cnt0):
    n = x2d.shape[0]
    tm = ROW_TILE
    row = lambda w: pl.BlockSpec((tm, w), lambda i: (i, 0))
    full = lambda a: pl.BlockSpec(a.shape, lambda i: (0,) * a.ndim)
    return pl.pallas_call(
        _outproj_body,
        grid=(n // tm,),
        in_specs=[row(WIDTH), row(WIDTH), row(D_MODEL), full(w_out), full(g2), full(wr), full(br), full(cnt0)],
        out_specs=(row(D_MODEL), row(D_MODEL), row(LANES), row(LANES), full(cnt0)),
        out_shape=(jax.ShapeDtypeStruct((n, D_MODEL), F32), jax.ShapeDtypeStruct((n, D_MODEL), BF16),
                   jax.ShapeDtypeStruct((n, LANES), F32), jax.ShapeDtypeStruct((n, LANES), jnp.int32),
                   jax.ShapeDtypeStruct(cnt0.shape, F32)),
        scratch_shapes=[pltpu.VMEM(cnt0.shape, F32)],
        compiler_params=pltpu.CompilerParams(dimension_semantics=("arbitrary",),
                                             vmem_limit_bytes=VMEM_LIMIT),
        name="outproj",
    )(oa, ob, x2d, w_out, g2, wr, br, cnt0)


def _moe_body(be_ref, nv_ref, x_ref, wg_ref, bg_ref, wu_ref, bu_ref, wd_ref, bdn_ref, y_ref,
              wg_s, wu_s, wd_s):
    i = pl.program_id(0)
    new_expert = jnp.logical_or(i == 0, be_ref[i] != be_ref[jnp.maximum(i - 1, 0)])

    @pl.when(new_expert)
    def _():
        wg_s[...] = wg_ref[0].astype(BF16)
        wu_s[...] = wu_ref[0].astype(BF16)
        wd_s[...] = wd_ref[0].astype(BF16)

    @pl.when(i < nv_ref[0])
    def _():
        x = x_ref[...]
        gate = jnp.minimum(_dot(x, wg_s[...]) + bg_ref[0], SWIGLU_LIMIT)
        up = jnp.clip(_dot(x, wu_s[...]) + bu_ref[0], -SWIGLU_LIMIT, SWIGLU_LIMIT)
        glu = gate * _sigmoid(SWIGLU_ALPHA * gate)
        hmid = ((up + 1.0) * glu).astype(BF16)
        y_ref[...] = _dot(hmid, wd_s[...]) + bdn_ref[0]

    @pl.when(i >= nv_ref[0])
    def _():
        y_ref[...] = jnp.zeros_like(y_ref)


def _moe(block_e, nvalid, xb, wg, bg, wu, bu, wd, bdn):
    n_rows = xb.shape[0]
    nb = n_rows // MOE_ROWS
    d_ff = wg.shape[2]
    wspec = lambda s: pl.BlockSpec((1,) + s, lambda i, be, nv: (be[i], 0, 0))
    grid_spec = pltpu.PrefetchScalarGridSpec(
        num_scalar_prefetch=2,
        grid=(nb,),
        in_specs=[pl.BlockSpec((MOE_ROWS, D_MODEL), lambda i, be, nv: (i, 0)),
                  wspec((D_MODEL, d_ff)), wspec((1, d_ff)), wspec((D_MODEL, d_ff)), wspec((1, d_ff)),
                  wspec((d_ff, D_MODEL)), wspec((1, D_MODEL))],
        out_specs=pl.BlockSpec((MOE_ROWS, D_MODEL), lambda i, be, nv: (i, 0)),
        scratch_shapes=[pltpu.VMEM((D_MODEL, d_ff), BF16), pltpu.VMEM((D_MODEL, d_ff), BF16),
                        pltpu.VMEM((d_ff, D_MODEL), BF16)],
    )
    return pl.pallas_call(
        _moe_body,
        grid_spec=grid_spec,
        out_shape=jax.ShapeDtypeStruct((n_rows, D_MODEL), F32),
        compiler_params=pltpu.CompilerParams(dimension_semantics=("arbitrary",),
                                             vmem_limit_bytes=VMEM_LIMIT),
        name="moe_ffn",
    )(block_e, nvalid, xb, wg, bg, wu, bu, wd, bdn)


def _blockdiag_state(s):
    b = s.shape[0]
    ng = N_HEADS // GROUP
    s5 = s.reshape(b, ng, GROUP, HEAD_DIM, HEAD_DIM)
    eye = jnp.eye(GROUP, dtype=s.dtype)
    return jnp.einsum("bghde,hk->bghdke", s5, eye).reshape(b, ng, GW, GW)


def _unblock_state(sbd):
    b = sbd.shape[0]
    ng = N_HEADS // GROUP
    s6 = sbd.reshape(b, ng, GROUP, HEAD_DIM, GROUP, HEAD_DIM)
    d = jnp.diagonal(s6, axis1=2, axis2=4)
    return jnp.moveaxis(d, -1, 2).reshape(b, N_HEADS, HEAD_DIM, HEAD_DIM)


def _mixer(x, cache_k, cache_v, page_table, s0, cbuf, rel_bias, lw):
    (g1, w_cat, bd, qg, kg, cw, av, dtb, eb, eg, og) = lw
    b, t, _ = x.shape
    x2d = x.reshape(b * t, D_MODEL)
    prompt = cache_k is None
    outs = _inproj(x2d, g1, w_cat, bd, qg, kg, seq_len=t, attn_layout=prompt)
    qkva, z, bg, qb, kb, vb = outs[:6]
    r3 = lambda a: a.reshape(b, t, a.shape[-1])
    qkva3 = r3(qkva)
    c = min(DELTA_CHUNK, t)
    tt = min(ROW_TILE, t)
    o_a, s_fin = _gdn(qkva3, r3(z), r3(bg), cbuf, _blockdiag_state(s0), cw, av, dtb, eb, eg, bd, og,
                      tt=tt, c=c)
    new_conv = jnp.concatenate([cbuf, qkva3], axis=1)[:, t:] if t < CONV_WIDTH - 1 else qkva3[:, t - (CONV_WIDTH - 1):]
    qb3 = r3(qb)
    rb = rel_bias.astype(F32)

    if prompt:
        kbf, vt, km = outs[6:]
        heads_last = lambda a: jnp.transpose(a.reshape(b, N_HEADS, HEAD_DIM, t), (0, 3, 1, 2))
        k4, v4 = heads_last(kb), heads_last(vb)
        nb = t // MOBA_BLOCK
        kmp = jnp.pad(km.reshape(b, nb, WIDTH), ((0, 0), (0, LANES - nb), (0, 0)))
        ii = np.arange(MOBA_BLOCK)
        d_own = ii[None, :] - ii[:, None]
        far_b = _bucket_np(np.arange(MOBA_BLOCK + 1, max(t, MOBA_BLOCK + 2)))
        assert (far_b == far_b[0]).all()
        bias_tab = _bias_table(rel_bias, np.concatenate([d_own, d_own + MOBA_BLOCK], axis=0))
        bias_tab = bias_tab.reshape(N_HEADS, 2, MOBA_BLOCK, MOBA_BLOCK)
        far = rb[int(far_b[0])]
        o_b = _moba_prompt(qb3, r3(kbf), vt, kmp, bias_tab, far).reshape(b * t, WIDTH)
    else:
        npages = page_table.shape[1]
        page = cache_k.shape[1]
        past = npages * page
        assert page == LANES and past % MOBA_BLOCK == 0 and t <= LANES
        n_past = past // MOBA_BLOCK
        pps = next(n for n in (32, 16, 8, 4, 2) if npages % n == 0)
        rows = N_HEADS * t
        pt_flat = page_table.reshape(-1).astype(jnp.int32)
        slab = lambda cch: jnp.transpose(cch, (0, 2, 3, 1)).reshape(cch.shape[0], WIDTH, page)
        kb3, vb3 = r3(kb), r3(vb)
        k4, v4 = (a.reshape(b, t, N_HEADS, HEAD_DIM) for a in (kb3, vb3))
        scores, kmt = _dec_scores(pt_flat, qb3, slab(cache_k), npages=npages, pps=pps)
        tpos = np.arange(t)
        d_past = past + tpos[:, None] - np.arange(past)[None, :]
        bias_past = _bias_table(rel_bias, d_past).reshape(rows, past)
        d_own = np.zeros((t, LANES), np.int64)
        d_own[:, :t] = tpos[:, None] - tpos[None, :]
        bias_own = _bias_table(rel_bias, d_own).reshape(rows, LANES)
        ex = np.zeros((LANES, past), np.float32)
        ex[np.arange(past) // MOBA_BLOCK, np.arange(past)] = 1.0
        probs, o_own = _dec_select(scores, kmt, qb3, kb3, vb3, bias_past, bias_own, jnp.asarray(ex, BF16),
                                   n_past=n_past)
        o_b = _dec_pv(pt_flat, probs, o_own, slab(cache_v), npages=npages, pps=pps, tq=t).reshape(b * t, WIDTH)
    return o_a.reshape(b * t, WIDTH), o_b, k4, v4, _unblock_state(s_fin), new_conv


def _moe_ffn(groups, counts, wg, bg, wu, bu, wd, bdn):
    xn = jnp.concatenate([g[0] for g in groups], axis=0)
    n = xn.shape[0]
    nk = n * TOP_K
    er = jnp.concatenate([g[2] for g in groups], axis=0)
    flat_e = er[:, :TOP_K].reshape(nk)
    rank = er[:, TOP_K:2 * TOP_K].reshape(nk)
    counts = counts[0, :N_EXPERTS].astype(jnp.int32)
    padded = ((counts + MOE_ROWS - 1) // MOE_ROWS) * MOE_ROWS
    pad_end = jnp.cumsum(padded)
    pad_start = pad_end - padded
    dest = pad_start[flat_e] + rank
    n_blocks = -(-nk // MOE_ROWS) + N_EXPERTS
    n_rows = n_blocks * MOE_ROWS
    flat_tok = jnp.arange(nk, dtype=jnp.int32) // TOP_K
    slot_tok = jnp.full((n_rows,), n, jnp.int32).at[dest].set(flat_tok, unique_indices=True)
    nvalid = (pad_end[-1] // MOE_ROWS).astype(jnp.int32).reshape(1)
    blk_start = jnp.arange(n_blocks, dtype=jnp.int32) * MOE_ROWS
    block_e = jnp.sum((blk_start[:, None] >= pad_end[None, :]).astype(jnp.int32), axis=1)
    block_e = jnp.minimum(block_e, N_EXPERTS - 1)
    last_e = block_e[jnp.maximum(nvalid[0] - 1, 0)]
    block_e = jnp.where(jnp.arange(n_blocks) < nvalid[0], block_e, last_e)
    x_ext = jnp.concatenate([xn, jnp.zeros((1, D_MODEL), xn.dtype)], axis=0)
    xb = x_ext[slot_tok]
    yb = _moe(block_e, nvalid, xb, wg, bg, wu, bu, wd, bdn)
    dest2 = dest.reshape(n, TOP_K)
    ys, row0 = [], 0
    for (_, gates, _, x1) in groups:
        ng = x1.shape[0]
        y = x1
        for kk in range(TOP_K):
            y = y + gates[:, kk:kk + 1] * yb[dest2[row0:row0 + ng, kk]]
        ys.append(y)
        row0 += ng
    return ys


def kernel(x_prompt, x_sample, cache_k, cache_v, state_delta, state_conv, page_table, rel_bias,
           norm1_g, w_in, conv_w, A_log, dt_bias, o_norm_g, q_norm_g, k_norm_g, w_out, norm2_g,
           w_router, b_router, w_gate, b_gate, w_up, b_up, w_down, b_down):
    depth = norm1_g.shape[0]
    bp, tp, _ = x_prompt.shape
    bs, ts, _ = x_sample.shape
    yp, ys = x_prompt, x_sample
    outs = [[] for _ in range(8)]
    seg = np.arange(WIDTH) // HEAD_DIM
    segw = np.arange(GW) // HEAD_DIM
    bd = jnp.asarray((segw[:, None] == segw[None, :]).astype(np.float32) / HEAD_DIM, BF16)
    eb_np = np.zeros((LANES, WIDTH), np.float32)
    eb_np[seg, np.arange(WIDTH)] = 1.0
    eg_np = np.zeros((LANES, WIDTH), np.float32)
    eg_np[N_HEADS + seg, np.arange(WIDTH)] = 1.0
    eb, eg = jnp.asarray(eb_np, BF16), jnp.asarray(eg_np, BF16)
    c1, c3 = QKV + WIDTH, QKV + WIDTH + 2 * N_HEADS
    for l in range(depth):
        wl = w_in[l]
        w_cat = jnp.concatenate([wl[:, :c1], wl[:, c3:], wl[:, c1:c3],
                                 jnp.zeros((D_MODEL, LANES - 2 * N_HEADS), wl.dtype)], axis=1).astype(BF16)
        tile8 = lambda g: jnp.tile(g.astype(F32), N_HEADS).reshape(1, WIDTH)
        av = jnp.zeros((1, LANES), F32).at[0, N_HEADS:2 * N_HEADS].set(-jnp.exp(A_log[l].astype(F32)))
        dtb = jnp.zeros((1, LANES), F32).at[0, N_HEADS:2 * N_HEADS].set(dt_bias[l].astype(F32))
        lw = (norm1_g[l].astype(F32).reshape(1, D_MODEL), w_cat, bd, tile8(q_norm_g[l]), tile8(k_norm_g[l]),
              conv_w[l].astype(F32), av, dtb, eb, eg, tile8(o_norm_g[l]))
        s0_p = jnp.zeros((bp, N_HEADS, HEAD_DIM, HEAD_DIM), F32)
        c0_p = jnp.zeros((bp, CONV_WIDTH - 1, QKV), F32)
        oa_p, ob_p, k_p, v_p, s_p, c_p = _mixer(yp, None, None, None, s0_p, c0_p, rel_bias, lw)
        oa_s, ob_s, k_s, v_s, s_s, c_s = _mixer(ys, cache_k[l], cache_v[l], page_table, state_delta[l],
                                                state_conv[l], rel_bias, lw)
        w_o = w_out[l].astype(BF16)
        g2 = norm2_g[l].astype(F32).reshape(1, D_MODEL)
        wr = jnp.pad(w_router[l].astype(F32), ((0, 0), (0, LANES - N_EXPERTS)))
        br = jnp.pad(b_router[l].astype(F32), (0, LANES - N_EXPERTS)).reshape(1, LANES)
        cnt0 = jnp.zeros((1, LANES), F32)
        x1_p, xn_p, gt_p, er_p, cnt_p = _outproj(oa_p, ob_p, yp.reshape(bp * tp, D_MODEL), w_o, g2, wr, br, cnt0)
        x1_s, xn_s, gt_s, er_s, cnt = _outproj(oa_s, ob_s, ys.reshape(bs * ts, D_MODEL), w_o, g2, wr, br, cnt_p)
        y_p, y_s = _moe_ffn([(xn_p, gt_p, er_p, x1_p), (xn_s, gt_s, er_s, x1_s)], cnt,
                            w_gate[l].astype(F32), b_gate[l].astype(F32)[:, None, :],
                            w_up[l].astype(F32), b_up[l].astype(F32)[:, None, :],
                            w_down[l].astype(F32), b_down[l].astype(F32)[:, None, :])
        yp = y_p.reshape(bp, tp, D_MODEL)
        ys = y_s.reshape(bs, ts, D_MODEL)
        for lst, val in zip(outs, (k_p, v_p, s_p, c_p, k_s, v_s, s_s, c_s)):
            lst.append(val)
    stacked = [jnp.stack(o) for o in outs]
    return (yp, ys, *stacked)
```

```python
import functools
import math

import numpy as np
import jax
import jax.numpy as jnp
from jax import lax
from jax.experimental import pallas as pl
from jax.experimental.pallas import tpu as pltpu

F32 = jnp.float32
BF16 = jnp.bfloat16

D_MODEL = 1024
N_HEADS = 8
HEAD_DIM = 64
WIDTH = N_HEADS * HEAD_DIM
QKV = 3 * WIDTH
CONV_WIDTH = 4
DELTA_CHUNK = 64
MOBA_BLOCK = 256
MOBA_TOP = 3
ATTN_SCALE = HEAD_DIM ** -0.5
NUM_BUCKETS = 32
MAX_DISTANCE = 128
N_EXPERTS = 32
TOP_K = 4
SWIGLU_LIMIT = 7.0
SWIGLU_ALPHA = 1.702
RMS_EPS = 1e-6

LANES = 128
SUBLANES = 8
ROW_TILE = 256
MOE_ROWS = 512
MOE_ROWS_SMALL = 128
NEG = -1e30
VMEM_LIMIT = 48 * 1024 * 1024

GROUP = 4
GW = GROUP * HEAD_DIM
NPAIR = WIDTH // LANES
FAR_BLOCKS = 4

_NN = (((1,), (0,)), ((), ()))
_NT = (((1,), (1,)), ((), ()))
_TN = (((0,), (0,)), ((), ()))


def _dot(a, b, dims=_NN):
    return lax.dot_general(a, b, dims, preferred_element_type=F32)


def _split2(a):
    hi = a.astype(BF16)
    lo = (a - hi.astype(F32)).astype(BF16)
    return hi, lo


def _split3(a):
    hi = a.astype(BF16)
    r = a - hi.astype(F32)
    mid = r.astype(BF16)
    lo = (r - mid.astype(F32)).astype(BF16)
    return hi, mid, lo


def _mm1(a, b, dims=_NN):
    return _dot(a.astype(BF16), b.astype(BF16), dims)


def _mm3(a, b, dims=_NN):
    ah, al = _split2(a)
    bh, bl = _split2(b)
    return _dot(ah, bh, dims) + (_dot(ah, bl, dims) + _dot(al, bh, dims))


def _mm_exact_rhs(a, b_bf16, dims=_NN):
    hi, mid, lo = _split3(a)
    return _dot(hi, b_bf16, dims) + (_dot(mid, b_bf16, dims) + _dot(lo, b_bf16, dims))


def _mm_exact_lhs(a_bf16, b, dims=_NN):
    hi, mid, lo = _split3(b)
    return _dot(a_bf16, hi, dims) + (_dot(a_bf16, mid, dims) + _dot(a_bf16, lo, dims))


def _seg_mean(x, bd):
    outs = []
    for g in range(x.shape[1] // GW):
        hi, lo = _split2(x[:, g * GW:(g + 1) * GW])
        outs.append(_dot(hi, bd) + _dot(lo, bd))
    return outs[0] if len(outs) == 1 else jnp.concatenate(outs, axis=1)


def _sigmoid(x):
    return 1.0 / (1.0 + jnp.exp(-x))


def _softplus(x):
    return jnp.maximum(x, 0.0) + jnp.log(1.0 + jnp.exp(-jnp.abs(x)))


def _inproj_body(x_ref, g_ref, w_ref, bd_ref, qg_ref, kg_ref,
                 qkva_ref, z_ref, bg_ref, qb_ref, kb_ref, vb_ref, *attn_refs):
    x = x_ref[...]
    ms = jnp.mean(x * x, axis=-1, keepdims=True)
    h = (x * lax.rsqrt(ms + RMS_EPS) * g_ref[...]).astype(BF16)

    def proj(lo, hi):
        return _dot(h, w_ref[:, lo:hi])

    bd = bd_ref[...]
    qkva_ref[...] = proj(0, QKV)
    z_ref[...] = proj(QKV, QKV + WIDTH)
    c = QKV + WIDTH
    q = proj(c, c + WIDTH)
    qb_ref[...] = q * lax.rsqrt(_seg_mean(q * q, bd) + RMS_EPS) * qg_ref[...]
    k = proj(c + WIDTH, c + 2 * WIDTH)
    kn = k * lax.rsqrt(_seg_mean(k * k, bd) + RMS_EPS) * kg_ref[...]
    v = proj(c + 2 * WIDTH, c + 3 * WIDTH)
    bg_ref[...] = proj(c + 3 * WIDTH, c + 3 * WIDTH + LANES)
    if attn_refs:
        kbf_ref, vt_ref, km_ref = attn_refs
        kb_ref[0] = kn.T
        vt = v.T
        vb_ref[0] = vt
        kbf_ref[...] = kn.astype(BF16)
        km_ref[0] = jnp.mean(kn, axis=0, keepdims=True)
        vtb = vt.astype(BF16)
        for pp in range(NPAIR):
            vt_ref[0, pp, 0] = vtb[pp * LANES:(pp + 1) * LANES, :]
    else:
        kb_ref[...] = kn
        vb_ref[...] = v


def _inproj(x2d, g1, w_cat, bd, qg, kg, *, seq_len, attn_layout):
    n = x2d.shape[0]
    tm = ROW_TILE
    nt = n // tm
    row = lambda w: pl.BlockSpec((tm, w), lambda i: (i, 0))
    full = lambda a: pl.BlockSpec(a.shape, lambda i: (0,) * a.ndim)
    out_shape = [
        jax.ShapeDtypeStruct((n, QKV), F32), jax.ShapeDtypeStruct((n, WIDTH), F32),
        jax.ShapeDtypeStruct((n, LANES), F32), jax.ShapeDtypeStruct((n, WIDTH), F32),
    ]
    out_specs = [row(QKV), row(WIDTH), row(LANES), row(WIDTH)]
    if not attn_layout:
        out_shape += [jax.ShapeDtypeStruct((n, WIDTH), F32)] * 2
        out_specs += [row(WIDTH)] * 2
    else:
        tpb = seq_len // tm
        kv_t = jax.ShapeDtypeStruct((n // seq_len, WIDTH, seq_len), F32)
        kv_spec = pl.BlockSpec((1, WIDTH, tm), lambda i: (i // tpb, 0, i % tpb))
        out_shape += [kv_t, kv_t]
        out_specs += [kv_spec, kv_spec]
        out_shape += [jax.ShapeDtypeStruct((n, WIDTH), BF16),
                      jax.ShapeDtypeStruct((n // seq_len, NPAIR, tpb, LANES, tm), BF16),
                      jax.ShapeDtypeStruct((nt, 1, WIDTH), F32)]
        out_specs += [row(WIDTH),
                      pl.BlockSpec((1, NPAIR, 1, LANES, tm), lambda i: (i // tpb, 0, i % tpb, 0, 0)),
                      pl.BlockSpec((1, 1, WIDTH), lambda i: (i, 0, 0))]
    return pl.pallas_call(
        _inproj_body,
        grid=(nt,),
        in_specs=[row(D_MODEL), full(g1), full(w_cat), full(bd), full(qg), full(kg)],
        out_specs=tuple(out_specs),
        out_shape=tuple(out_shape),
        compiler_params=pltpu.CompilerParams(dimension_semantics=("parallel",),
                                             vmem_limit_bytes=VMEM_LIMIT),
        name="inproj",
    )(x2d, g1, w_cat, bd, qg, kg)


def _stack(x, masks):
    return jnp.concatenate([jnp.where(m, x, 0.0) for m in masks], axis=0)


def _unstack(w, c, n):
    acc = w[0:c]
    for h in range(1, n):
        acc = acc + w[h * c:(h + 1) * c]
    return acc


def _gdn_body(qkva_ref, z_ref, bg_ref, cbuf_ref, s0_ref, cw_ref, av_ref, dtb_ref, eb_ref, eg_ref,
              bd_ref, og_ref, o_ref, sfin_ref, s_scr, carry_scr, uext_scr, *, tt, c):
    t = pl.program_id(1)
    nt = pl.num_programs(1)
    gc = GROUP * c

    @pl.when(t == 0)
    def _():
        s_scr[...] = s0_ref[0]
        carry_scr[...] = jnp.zeros_like(carry_scr)
        carry_scr[SUBLANES - (CONV_WIDTH - 1):SUBLANES, :] = cbuf_ref[0]

    uext_scr[0:SUBLANES, :] = carry_scr[...]
    uext_scr[SUBLANES:SUBLANES + tt, :] = qkva_ref[0]
    carry_scr[...] = uext_scr[tt:tt + SUBLANES, :]

    cw = cw_ref[...]
    off = SUBLANES - (CONV_WIDTH - 1)
    conv = cw[0:1] * uext_scr[off:off + tt, :]
    for i in range(1, CONV_WIDTH):
        conv = conv + cw[i:i + 1] * uext_scr[off + i:off + i + tt, :]
    act = conv * _sigmoid(conv)

    bd = bd_ref[...]
    q = act[:, 0:WIDTH]
    k = act[:, WIDTH:2 * WIDTH]
    v = act[:, 2 * WIDTH:3 * WIDTH]
    q = q * lax.rsqrt(_seg_mean(q * q, bd) * HEAD_DIM + RMS_EPS) * (HEAD_DIM ** -0.5)
    k = k * lax.rsqrt(_seg_mean(k * k, bd) * HEAD_DIM + RMS_EPS)

    bg = bg_ref[0]
    beta = _mm_exact_rhs(_sigmoid(bg), eb_ref[...])
    gx = _mm_exact_rhs(av_ref[...] * _softplus(bg + dtb_ref[...]), eg_ref[...])

    r4 = lax.broadcasted_iota(jnp.int32, (gc, gc), 0)
    c4 = lax.broadcasted_iota(jnp.int32, (gc, gc), 1)
    same = (r4 // c) == (c4 // c)
    tri = jnp.logical_and(same, r4 >= c4)
    strict = jnp.logical_and(same, r4 > c4)
    eye = (r4 == c4).astype(F32)
    rc = lax.broadcasted_iota(jnp.int32, (c, c), 0)
    cc = lax.broadcasted_iota(jnp.int32, (c, c), 1)
    tril_c = (rc >= cc).astype(BF16)
    lane_g = lax.broadcasted_iota(jnp.int32, (1, GW), 1) // HEAD_DIM
    hmasks = [lane_g == h for h in range(GROUP)]
    rs = lax.broadcasted_iota(jnp.int32, (GW, GW), 0) // HEAD_DIM
    cs = lax.broadcasted_iota(jnp.int32, (GW, GW), 1) // HEAD_DIM
    bdm = (rs == cs).astype(F32)
    nsq = int(round(math.log2(c))) - 1

    nch, ngr = tt // c, N_HEADS // GROUP
    inst = []
    for ch in range(nch):
        r0 = ch * c
        gcum = _mm_exact_lhs(tril_c, gx[r0:r0 + c])
        eg = jnp.exp(gcum)
        glast = gcum[c - 1:c, :]
        kscale = jnp.exp(glast - gcum)
        gtot = jnp.exp(glast)
        for gr in range(ngr):
            sl = slice(gr * GW, (gr + 1) * GW)
            kq = k[r0:r0 + c, sl]
            qq = q[r0:r0 + c, sl]
            bb = beta[r0:r0 + c, sl]
            gq = gcum[:, sl]
            kb = kq * bb
            xk = _stack(kq, hmasks).astype(BF16)
            kk = _dot(_stack(kb, hmasks).astype(BF16), xk, _NT)
            qk = _dot(_stack(qq, hmasks).astype(BF16), xk, _NT)
            gcol = jnp.concatenate(
                [jnp.broadcast_to(gq[:, h * HEAD_DIM:h * HEAD_DIM + 1], (c, gc)) for h in range(GROUP)], axis=0)
            if gc % LANES == 0:
                grow = gcol.T
            else:
                grow = _mm_exact_lhs(jnp.full((gc, GW), 1.0 / HEAD_DIM, BF16), _stack(gq, hmasks), _NT)
            dm = jnp.where(tri, jnp.exp(jnp.where(tri, gcol - grow, 0.0)), 0.0)
            lmat = jnp.where(strict, kk * dm, 0.0)
            rhs = jnp.concatenate([_stack(v[r0:r0 + c, sl] * bb, hmasks), _stack(kb * eg[:, sl], hmasks)], axis=1)
            inst.append(dict(gr=gr, sl=sl, amat=qk * dm, pinv=eye - lmat, msq=lmat, rhs=rhs,
                             qe=qq * eg[:, sl], kd=kq * kscale[:, sl], gtot=gtot[:, sl]))
    for _ in range(nsq):
        for it in inst:
            it["msq"] = _mm1(it["msq"], it["msq"])
        for it in inst:
            it["pinv"] = it["pinv"] + _mm1(it["pinv"], it["msq"])
    for it in inst:
        w = _mm1(it["pinv"], it["rhs"])
        it["value"] = _unstack(w[:, 0:GW], c, GROUP)
        it["kcum"] = _unstack(w[:, GW:2 * GW], c, GROUP)
    o_rows = []
    for ch in range(nch):
        o_groups = []
        for it in inst[ch * ngr:(ch + 1) * ngr]:
            s = s_scr[it["gr"]]
            u = it["value"] - _mm3(it["kcum"], s)
            o_groups.append(_mm1(it["qe"], s) + _unstack(_mm1(it["amat"], _stack(u, hmasks)), c, GROUP))
            s_scr[it["gr"]] = s * it["gtot"] + bdm * _mm3(it["kd"], u, _TN)
        o_rows.append(jnp.concatenate(o_groups, axis=1))
    o = o_rows[0] if len(o_rows) == 1 else jnp.concatenate(o_rows, axis=0)
    on = o * lax.rsqrt(_seg_mean(o * o, bd) + RMS_EPS) * og_ref[...]
    z = z_ref[0]
    o_ref[0] = on * (z * _sigmoid(z))

    @pl.when(t == nt - 1)
    def _():
        sfin_ref[0] = s_scr[...]


def _gdn(qkva, z, bg, cbuf, s0bd, cw, av, dtb, eb, eg, bd, og, *, tt, c):
    b, t, _ = qkva.shape
    ng = N_HEADS // GROUP
    full = lambda a: pl.BlockSpec(a.shape, lambda i, j: (0,) * a.ndim)
    body = functools.partial(_gdn_body, tt=tt, c=c)
    return pl.pallas_call(
        body,
        grid=(b, t // tt),
        in_specs=[
            pl.BlockSpec((1, tt, QKV), lambda i, j: (i, j, 0)),
            pl.BlockSpec((1, tt, WIDTH), lambda i, j: (i, j, 0)),
            pl.BlockSpec((1, tt, LANES), lambda i, j: (i, j, 0)),
            pl.BlockSpec((1, CONV_WIDTH - 1, QKV), lambda i, j: (i, 0, 0)),
            pl.BlockSpec((1, ng, GW, GW), lambda i, j: (i, 0, 0, 0)),
            full(cw), full(av), full(dtb), full(eb), full(eg), full(bd), full(og),
        ],
        out_specs=(pl.BlockSpec((1, tt, WIDTH), lambda i, j: (i, j, 0)),
                   pl.BlockSpec((1, ng, GW, GW), lambda i, j: (i, 0, 0, 0))),
        out_shape=(jax.ShapeDtypeStruct((b, t, WIDTH), F32),
                   jax.ShapeDtypeStruct((b, ng, GW, GW), F32)),
        scratch_shapes=[pltpu.VMEM((ng, GW, GW), F32), pltpu.VMEM((SUBLANES, QKV), F32),
                        pltpu.VMEM((tt + SUBLANES, QKV), F32)],
        compiler_params=pltpu.CompilerParams(dimension_semantics=("arbitrary", "arbitrary"),
                                             vmem_limit_bytes=VMEM_LIMIT),
        name="gdn",
    )(qkva, z, bg, cbuf, s0bd, cw, av, dtb, eb, eg, bd, og)


def _bucket_np(dist):
    n = np.maximum(dist, 0)
    max_exact = NUM_BUCKETS // 2
    n_f = np.maximum(n, max_exact).astype(np.float32)
    large = max_exact + (np.log(n_f / np.float32(max_exact)) / np.float32(math.log(MAX_DISTANCE / max_exact))
                         * np.float32(NUM_BUCKETS - max_exact)).astype(np.int32)
    large = np.minimum(large, NUM_BUCKETS - 1)
    return np.where(n < max_exact, n, large).astype(np.int32)


def _bucket_steps(max_dist):
    b = _bucket_np(np.arange(max_dist + 1))
    assert (np.diff(b) >= 0).all()
    firsts = np.flatnonzero(np.diff(b, prepend=-1))
    return [(int(d), int(b[d])) for d in firsts]


def _bias_body(rb_ref, d_ref, o_ref, *, steps):
    h = pl.program_id(0)
    d = d_ref[...]
    acc = jnp.full(d.shape, rb_ref[steps[0][1] * N_HEADS + h], F32)
    for first, bucket in steps[1:]:
        acc = jnp.where(d >= first, rb_ref[bucket * N_HEADS + h], acc)
    o_ref[0] = acc


def _bias_table(rel_bias, dist_np):
    r, c = dist_np.shape
    steps = _bucket_steps(int(max(dist_np.max(), 0)))
    grid_spec = pltpu.PrefetchScalarGridSpec(
        num_scalar_prefetch=1,
        grid=(N_HEADS,),
        in_specs=[pl.BlockSpec((r, c), lambda h, rb: (0, 0))],
        out_specs=pl.BlockSpec((1, r, c), lambda h, rb: (h, 0, 0)),
    )
    return pl.pallas_call(
        functools.partial(_bias_body, steps=steps),
        grid_spec=grid_spec,
        out_shape=jax.ShapeDtypeStruct((N_HEADS, r, c), F32),
        compiler_params=pltpu.CompilerParams(dimension_semantics=("arbitrary",),
                                             vmem_limit_bytes=VMEM_LIMIT),
        name="bias_table",
    )(rel_bias.astype(F32).reshape(-1), jnp.asarray(np.maximum(dist_np, 0), jnp.int32))


def _top_select(gate, idx, n_cand, n_valid_f, axis):
    g = jnp.where(idx < n_cand, gate, -jnp.inf)
    sel = jnp.zeros(gate.shape, F32)
    for r in range(MOBA_TOP):
        mx = jnp.max(g, axis=axis, keepdims=True)
        first = jnp.min(jnp.where(g == mx, idx, jnp.int32(1 << 30)), axis=axis, keepdims=True)
        hit = idx == first
        sel = jnp.maximum(sel, jnp.where(hit, n_valid_f[r], 0.0))
        g = jnp.where(hit, -jnp.inf, g)
    return sel


def _moba_prompt_body(far_ref, q_ref, k_ref, vt_ref, km_ref, bias_ref, o_ref, sel_scr):
    p = pl.program_id(1)
    i = pl.program_id(2)
    blk = MOBA_BLOCK
    qt = (q_ref[0] * ATTN_SCALE).T
    row = lax.broadcasted_iota(jnp.int32, (LANES, blk), 0)
    top = row < HEAD_DIM
    qt_m = [jnp.where(top, qt, 0.0), jnp.where(top, 0.0, qt)]
    qtb = [x.astype(BF16) for x in qt_m]
    km = km_ref[0]
    n_valid = [(i > r).astype(F32) for r in range(MOBA_TOP)]
    for hh in range(2):
        sel_scr[hh] = _top_select(_mm3(km, qt_m[hh]), row, i, n_valid, 0)
    rowk = lax.broadcasted_iota(jnp.int32, (blk, blk), 0)
    colq = lax.broadcasted_iota(jnp.int32, (blk, blk), 1)

    def scores(n, nblk):
        kblks = [k_ref[0, pl.ds(pl.multiple_of((n + j) * blk, blk), blk), :] for j in range(nblk)]
        return [[_dot(kb_, qtb[hh]) for kb_ in kblks] for hh in range(2)]

    def update(n, carry, modes, raw):
        m0, l0, m1, l1, acc = carry
        ms, ls = [m0, m1], [l0, l1]
        vts = [vt_ref[0, 0, n + j] for j in range(len(modes))]
        alphas, pvs = [], []
        for hh in range(2):
            ss, shifts, valids, bmax = [], [], [], []
            for j, mode in enumerate(modes):
                s = raw[hh][j]
                if mode == "own":
                    s = jnp.where(rowk <= colq, s + bias_ref[hh, 0], NEG)
                    shift, valid = 0.0, None
                else:
                    if mode == "prev":
                        s, shift = s + bias_ref[hh, 1], 0.0
                    else:
                        shift = far_ref[2 * p + hh]
                    valid = sel_scr[hh, pl.ds(n + j, 1), :] > 0.5
                bm = jnp.max(s, axis=0, keepdims=True) + shift
                ss.append(s)
                shifts.append(shift)
                valids.append(valid)
                bmax.append(bm if valid is None else jnp.where(valid, bm, NEG))
            m_new = ms[hh]
            for bm in bmax:
                m_new = jnp.maximum(m_new, bm)
            alpha = jnp.exp(ms[hh] - m_new)
            lsum = alpha * ls[hh]
            pv = None
            for j, s in enumerate(ss):
                off = m_new - shifts[j]
                if valids[j] is not None:
                    off = jnp.where(valids[j], off, -NEG)
                pr = jnp.exp(s - off)
                lsum = lsum + jnp.sum(pr, axis=0, keepdims=True)
                d = _dot(vts[j], pr.astype(BF16))
                pv = d if pv is None else pv + d
            ls[hh] = lsum
            ms[hh] = m_new
            alphas.append(alpha)
            pvs.append(pv)
        acc = acc * jnp.where(top, alphas[0], alphas[1]) + jnp.where(top, pvs[0], pvs[1])
        return ms[0], ls[0], ms[1], ls[1], acc

    init = (jnp.full((1, blk), NEG, F32), jnp.zeros((1, blk), F32),
            jnp.full((1, blk), NEG, F32), jnp.zeros((1, blk), F32),
            jnp.zeros((LANES, blk), F32))
    def step(n, carry, modes):
        return update(n, carry, modes, scores(n, len(modes)))

    n_far = jnp.maximum(i - 1, 0)
    n_wide = n_far // FAR_BLOCKS
    wide = ("far",) * FAR_BLOCKS
    n_blocks = k_ref.shape[1] // blk
    last_start = max(n_blocks - FAR_BLOCKS, 0)

    def wide_step(w, cr):
        soft, raw = cr
        nxt = scores(jnp.clip((w + 1) * FAR_BLOCKS, 0, last_start), FAR_BLOCKS)
        return update(w * FAR_BLOCKS, soft, wide, raw), nxt

    if n_blocks >= FAR_BLOCKS:
        carry, _ = lax.fori_loop(0, n_wide, wide_step, (init, scores(0, FAR_BLOCKS)))
    else:
        carry = init
    done = n_wide * FAR_BLOCKS
    width = FAR_BLOCKS // 2
    while width >= 1:
        take = ((n_far - done) >= width).astype(jnp.int32)
        carry = lax.fori_loop(0, take, lambda _, cr, d=done, w=width: step(d, cr, ("far",) * w), carry)
        done = done + take * width
        width //= 2
    has_prev = (i >= 1).astype(jnp.int32)
    carry = lax.fori_loop(0, has_prev, lambda _, cr: step(i - 1, cr, ("prev", "own")), carry)
    m0, l0, m1, l1, acc = lax.fori_loop(0, 1 - has_prev, lambda _, cr: step(i, cr, ("own",)), carry)
    o_ref[0] = (acc / jnp.where(top, l0, l1)).T


def _moba_prompt(qb, kbf, vt, kmp, bias_tab, far):
    b, t, _ = qb.shape
    nq = t // MOBA_BLOCK
    grid_spec = pltpu.PrefetchScalarGridSpec(
        num_scalar_prefetch=1,
        grid=(b, NPAIR, nq),
        in_specs=[
            pl.BlockSpec((1, MOBA_BLOCK, LANES), lambda bi, p, i, far: (bi, i, p)),
            pl.BlockSpec((1, t, LANES), lambda bi, p, i, far: (bi, 0, p)),
            pl.BlockSpec((1, 1, nq, LANES, MOBA_BLOCK), lambda bi, p, i, far: (bi, p, 0, 0, 0)),
            pl.BlockSpec((1, LANES, LANES), lambda bi, p, i, far: (bi, 0, p)),
            pl.BlockSpec((2, 2, MOBA_BLOCK, MOBA_BLOCK), lambda bi, p, i, far: (p, 0, 0, 0)),
        ],
        out_specs=pl.BlockSpec((1, MOBA_BLOCK, LANES), lambda bi, p, i, far: (bi, i, p)),
        scratch_shapes=[pltpu.VMEM((2, LANES, MOBA_BLOCK), F32)],
    )
    return pl.pallas_call(
        _moba_prompt_body,
        grid_spec=grid_spec,
        out_shape=jax.ShapeDtypeStruct((b, t, WIDTH), F32),
        compiler_params=pltpu.CompilerParams(
            dimension_semantics=("arbitrary", "arbitrary", "arbitrary"),
            vmem_limit_bytes=VMEM_LIMIT),
        name="moba_prompt",
    )(far, qb, kbf, vt, kmp, bias_tab)


def _head_masks(width):
    lane_h = lax.broadcasted_iota(jnp.int32, (1, width), 1) // HEAD_DIM
    return [lane_h == h for h in range(width // HEAD_DIM)]


def _dec_scores_body(pt_ref, q_ref, *refs, pps):
    k_refs = refs[:pps]
    s_ref, kmt_ref = refs[pps], refs[pps + 1]
    masks = _head_masks(WIDTH)
    qs = _stack(q_ref[0] * ATTN_SCALE, masks).astype(BF16)
    ppb = MOBA_BLOCK // LANES
    j = pl.program_id(1)
    for i in range(pps):
        s_ref[0, :, i * LANES:(i + 1) * LANES] = _dot(qs, k_refs[i][0].astype(BF16))
    lane = lax.broadcasted_iota(jnp.int32, (1, LANES), 1)
    @pl.when(j == 0)
    def _():
        kmt_ref[0] = jnp.zeros((WIDTH, LANES), F32)

    kmt = kmt_ref[0]
    for blk in range(pps // ppb):
        tot = k_refs[blk * ppb][0]
        for jj in range(1, ppb):
            tot = tot + k_refs[blk * ppb + jj][0]
        mean = jnp.sum(tot, axis=1, keepdims=True) * (1.0 / MOBA_BLOCK)
        kmt = jnp.where(lane == j * (pps // ppb) + blk, mean, kmt)
    kmt_ref[0] = kmt


def _page_specs(pps, npages):
    def mk(i):
        return pl.BlockSpec((1, WIDTH, LANES), lambda b, j, pt: (pt[b * npages + j * pps + i], 0, 0))
    return [mk(i) for i in range(pps)]


def _dec_scores(pt_flat, q, cache, *, npages, pps):
    b, tq, _ = q.shape
    ppb = MOBA_BLOCK // LANES
    nj = npages // pps
    grid_spec = pltpu.PrefetchScalarGridSpec(
        num_scalar_prefetch=1,
        grid=(b, nj),
        in_specs=[pl.BlockSpec((1, tq, WIDTH), lambda bi, j, pt: (bi, 0, 0))] + _page_specs(pps, npages),
        out_specs=(pl.BlockSpec((1, N_HEADS * tq, pps * LANES), lambda bi, j, pt: (bi, 0, j)),
                   pl.BlockSpec((1, WIDTH, LANES), lambda bi, j, pt: (bi, 0, 0))),
    )
    assert npages // ppb <= LANES
    return pl.pallas_call(
        functools.partial(_dec_scores_body, pps=pps),
        grid_spec=grid_spec,
        out_shape=(jax.ShapeDtypeStruct((b, N_HEADS * tq, npages * LANES), F32),
                   jax.ShapeDtypeStruct((b, WIDTH, LANES), F32)),
        compiler_params=pltpu.CompilerParams(dimension_semantics=("arbitrary", "arbitrary"),
                                             vmem_limit_bytes=VMEM_LIMIT),
        name="dec_scores",
    )(pt_flat, q, *([cache] * pps))


def _dec_select_body(s_ref, km_ref, q_ref, kn_ref, vn_ref, bp_ref, bo_ref, ex_ref, p_ref, oo_ref,
                     *, tq, n_past):
    rows = N_HEADS * tq
    masks = _head_masks(WIDTH)
    qs = _stack(q_ref[0] * ATTN_SCALE, masks)
    col = lax.broadcasted_iota(jnp.int32, (rows, LANES), 1)
    n_valid = [jnp.float32(1.0 if n_past > r else 0.0) for r in range(MOBA_TOP)]
    sel = _top_select(_mm3(qs, km_ref[0]), col, n_past, n_valid, -1)
    selk = _dot(sel.astype(BF16), ex_ref[...])
    logit = jnp.where(selk > 0.5, s_ref[0] + bp_ref[...], NEG)
    pad = jnp.zeros((LANES - tq, WIDTH), F32)
    kn = jnp.concatenate([kn_ref[0], pad], axis=0).astype(BF16)
    vn = jnp.concatenate([vn_ref[0], pad], axis=0).astype(BF16)
    trow = lax.rem(lax.broadcasted_iota(jnp.int32, (rows, LANES), 0), tq)
    s_own = jnp.where(col <= trow, _dot(qs.astype(BF16), kn, _NT) + bo_ref[...], NEG)
    m = jnp.maximum(jnp.max(logit, axis=-1, keepdims=True), jnp.max(s_own, axis=-1, keepdims=True))
    pr = jnp.exp(logit - m)
    po = jnp.exp(s_own - m)
    inv = 1.0 / (jnp.sum(pr, axis=-1, keepdims=True) + jnp.sum(po, axis=-1, keepdims=True))
    p_ref[0] = (pr * inv).astype(BF16)
    oo_ref[0] = _dot((po * inv).astype(BF16), vn)


def _dec_select(scores, kmp, q, kn, vn, bias_past, bias_own, expand, *, n_past):
    b, tq, _ = q.shape
    rows = N_HEADS * tq
    plen = scores.shape[2]
    full = lambda a: pl.BlockSpec(a.shape, lambda i: (0,) * a.ndim)
    per = lambda a: pl.BlockSpec((1,) + a.shape[1:], lambda i: (i,) + (0,) * (a.ndim - 1))
    return pl.pallas_call(
        functools.partial(_dec_select_body, tq=tq, n_past=n_past),
        grid=(b,),
        in_specs=[per(scores), per(kmp), per(q), per(kn), per(vn), full(bias_past), full(bias_own),
                  full(expand)],
        out_specs=(pl.BlockSpec((1, rows, plen), lambda i: (i, 0, 0)),
                   pl.BlockSpec((1, rows, WIDTH), lambda i: (i, 0, 0))),
        out_shape=(jax.ShapeDtypeStruct((b, rows, plen), BF16),
                   jax.ShapeDtypeStruct((b, rows, WIDTH), F32)),
        compiler_params=pltpu.CompilerParams(dimension_semantics=("arbitrary",),
                                             vmem_limit_bytes=VMEM_LIMIT),
        name="dec_select",
    )(scores, kmp, q, kn, vn, bias_past, bias_own, expand)


def _dec_pv_body(pt_ref, p_ref, oo_ref, *refs, pps, tq):
    v_refs = refs[:pps]
    o_ref, acc = refs[pps], refs[pps + 1]
    j = pl.program_id(1)

    @pl.when(j == 0)
    def _():
        acc[...] = oo_ref[0]

    tot = acc[...]
    for i in range(pps):
        tot = tot + _dot(p_ref[0, :, i * LANES:(i + 1) * LANES], v_refs[i][0].astype(BF16), _NT)
    acc[...] = tot

    @pl.when(j == pl.num_programs(1) - 1)
    def _():
        masks = _head_masks(WIDTH)
        a = acc[...]
        out = jnp.where(masks[0], a[0:tq], 0.0)
        for h in range(1, N_HEADS):
            out = out + jnp.where(masks[h], a[h * tq:(h + 1) * tq], 0.0)
        o_ref[0] = out


def _dec_pv(pt_flat, probs, o_own, cache, *, npages, pps, tq):
    b, rows, _ = probs.shape
    grid_spec = pltpu.PrefetchScalarGridSpec(
        num_scalar_prefetch=1,
        grid=(b, npages // pps),
        in_specs=[pl.BlockSpec((1, rows, pps * LANES), lambda bi, j, pt: (bi, 0, j)),
                  pl.BlockSpec((1, rows, WIDTH), lambda bi, j, pt: (bi, 0, 0))] + _page_specs(pps, npages),
        out_specs=pl.BlockSpec((1, tq, WIDTH), lambda bi, j, pt: (bi, 0, 0)),
        scratch_shapes=[pltpu.VMEM((rows, WIDTH), F32)],
    )
    return pl.pallas_call(
        functools.partial(_dec_pv_body, pps=pps, tq=tq),
        grid_spec=grid_spec,
        out_shape=jax.ShapeDtypeStruct((b, tq, WIDTH), F32),
        compiler_params=pltpu.CompilerParams(dimension_semantics=("arbitrary", "arbitrary"),
                                             vmem_limit_bytes=VMEM_LIMIT),
        name="dec_pv",
    )(pt_flat, probs, o_own, *([cache] * pps))


def _outproj_body(oa_ref, ob_ref, x_ref, w_ref, g_ref, wr_ref, br_ref, cnt0_ref,
                  x1_ref, xn_ref, gate_ref, exp_ref, cnt_ref, cnt_scr):
    @pl.when(pl.program_id(0) == 0)
    def _():
        cnt_scr[...] = cnt0_ref[...]

    mixed = _dot(oa_ref[...].astype(BF16), w_ref[0:WIDTH, :]) + _dot(ob_ref[...].astype(BF16),
                                                                      w_ref[WIDTH:2 * WIDTH, :])
    x1 = x_ref[...] + mixed
    x1_ref[...] = x1
    ms = jnp.mean(x1 * x1, axis=-1, keepdims=True)
    xn = x1 * lax.rsqrt(ms + RMS_EPS) * g_ref[...]
    xn_ref[...] = xn.astype(BF16)
    logits = _mm3(xn, wr_ref[...]) + br_ref[...]
    col = lax.broadcasted_iota(jnp.int32, logits.shape, 1)
    g = jnp.where(col < N_EXPERTS, logits, -jnp.inf)
    vals, idxs = [], []
    for _ in range(TOP_K):
        mx = jnp.max(g, axis=-1, keepdims=True)
        idx = jnp.min(jnp.where(g == mx, col, jnp.int32(1 << 30)), axis=-1, keepdims=True)
        vals.append(mx)
        idxs.append(idx)
        g = jnp.where(col == idx, -jnp.inf, g)
    es = [jnp.exp(vv - vals[0]) for vv in vals]
    den = es[0]
    for e in es[1:]:
        den = den + e
    tm = logits.shape[0]
    onehot = jnp.zeros(logits.shape, F32)
    for kk in range(TOP_K):
        onehot = onehot + (col == idxs[kk]).astype(F32)
    r_i = lax.broadcasted_iota(jnp.int32, (tm, tm), 0)
    c_i = lax.broadcasted_iota(jnp.int32, (tm, tm), 1)
    prefix = _dot((r_i > c_i).astype(BF16), onehot.astype(BF16)) + cnt_scr[...]
    gates = jnp.zeros(logits.shape, F32)
    experts = jnp.zeros(logits.shape, jnp.int32)
    for kk in range(TOP_K):
        gates = jnp.where(col == kk, es[kk] / den, gates)
        experts = jnp.where(col == kk, idxs[kk], experts)
        rank = jnp.sum(jnp.where(col == idxs[kk], prefix, 0.0), axis=-1, keepdims=True)
        experts = jnp.where(col == TOP_K + kk, rank.astype(jnp.int32), experts)
    gate_ref[...] = gates
    exp_ref[...] = experts
    cnt_scr[...] = cnt_scr[...] + jnp.sum(onehot, axis=0, keepdims=True)
    cnt_ref[...] = cnt_scr[...]


def _outproj(oa, ob, x2d, w_out, g2, wr, br, cnt0):
    n = x2d.shape[0]
    tm = ROW_TILE
    row = lambda w: pl.BlockSpec((tm, w), lambda i: (i, 0))
    full = lambda a: pl.BlockSpec(a.shape, lambda i: (0,) * a.ndim)
    return pl.pallas_call(
        _outproj_body,
        grid=(n // tm,),
        in_specs=[row(WIDTH), row(WIDTH), row(D_MODEL), full(w_out), full(g2), full(wr), full(br), full(cnt0)],
        out_specs=(row(D_MODEL), row(D_MODEL), row(LANES), row(LANES), full(cnt0)),
        out_shape=(jax.ShapeDtypeStruct((n, D_MODEL), F32), jax.ShapeDtypeStruct((n, D_MODEL), BF16),
                   jax.ShapeDtypeStruct((n, LANES), F32), jax.ShapeDtypeStruct((n, LANES), jnp.int32),
                   jax.ShapeDtypeStruct(cnt0.shape, F32)),
        scratch_shapes=[pltpu.VMEM(cnt0.shape, F32)],
        compiler_params=pltpu.CompilerParams(dimension_semantics=("arbitrary",),
                                             vmem_limit_bytes=VMEM_LIMIT),
        name="outproj",
    )(oa, ob, x2d, w_out, g2, wr, br, cnt0)


def _moe_body(be_ref, nv_ref, x_ref, wg_ref, bg_ref, wu_ref, bu_ref, wd_ref, bdn_ref, after_ref, y_ref,
              wg_s, wu_s, wd_s):
    i = pl.program_id(0)
    new_expert = jnp.logical_or(i == 0, be_ref[i] != be_ref[jnp.maximum(i - 1, 0)])

    @pl.when(new_expert)
    def _():
        wg_s[...] = wg_ref[0].astype(BF16)
        wu_s[...] = wu_ref[0].astype(BF16)
        wd_s[...] = wd_ref[0].astype(BF16)

    @pl.when(i < nv_ref[0])
    def _():
        x = x_ref[...]
        gate = jnp.minimum(_dot(x, wg_s[...]) + bg_ref[0], SWIGLU_LIMIT)
        up = jnp.clip(_dot(x, wu_s[...]) + bu_ref[0], -SWIGLU_LIMIT, SWIGLU_LIMIT)
        glu = gate * _sigmoid(SWIGLU_ALPHA * gate)
        hmid = ((up + 1.0) * glu).astype(BF16)
        y_ref[...] = _dot(hmid, wd_s[...]) + bdn_ref[0]

    @pl.when(i >= nv_ref[0])
    def _():
        y_ref[...] = jnp.zeros_like(y_ref)


def _moe(block_e, nvalid, xb, wg, bg, wu, bu, wd, bdn, after, *, rows):
    n_rows = xb.shape[0]
    nb = n_rows // rows
    d_ff = wg.shape[2]
    wspec = lambda s: pl.BlockSpec((1,) + s, lambda i, be, nv: (be[i], 0, 0))
    grid_spec = pltpu.PrefetchScalarGridSpec(
        num_scalar_prefetch=2,
        grid=(nb,),
        in_specs=[pl.BlockSpec((rows, D_MODEL), lambda i, be, nv: (i, 0)),
                  wspec((D_MODEL, d_ff)), wspec((1, d_ff)), wspec((D_MODEL, d_ff)), wspec((1, d_ff)),
                  wspec((d_ff, D_MODEL)), wspec((1, D_MODEL)),
                  pl.BlockSpec(after.shape, lambda i, be, nv: (0, 0))],
        out_specs=pl.BlockSpec((rows, D_MODEL), lambda i, be, nv: (i, 0)),
        scratch_shapes=[pltpu.VMEM((D_MODEL, d_ff), BF16), pltpu.VMEM((D_MODEL, d_ff), BF16),
                        pltpu.VMEM((d_ff, D_MODEL), BF16)],
    )
    return pl.pallas_call(
        _moe_body,
        grid_spec=grid_spec,
        out_shape=jax.ShapeDtypeStruct((n_rows, D_MODEL), F32),
        compiler_params=pltpu.CompilerParams(dimension_semantics=("arbitrary",),
                                             vmem_limit_bytes=VMEM_LIMIT),
        name="moe_ffn",
    )(block_e, nvalid, xb, wg, bg, wu, bu, wd, bdn, after)


def _blockdiag_state(s):
    b = s.shape[0]
    ng = N_HEADS // GROUP
    s5 = s.reshape(b, ng, GROUP, HEAD_DIM, HEAD_DIM)
    eye = jnp.eye(GROUP, dtype=s.dtype)
    return jnp.einsum("bghde,hk->bghdke", s5, eye).reshape(b, ng, GW, GW)


def _unblock_state(sbd):
    b = sbd.shape[0]
    ng = N_HEADS // GROUP
    s6 = sbd.reshape(b, ng, GROUP, HEAD_DIM, GROUP, HEAD_DIM)
    d = jnp.diagonal(s6, axis1=2, axis2=4)
    return jnp.moveaxis(d, -1, 2).reshape(b, N_HEADS, HEAD_DIM, HEAD_DIM)


def _mixer(x, cache_k, cache_v, page_table, s0, cbuf, rel_bias, lw):
    (g1, w_cat, bd, qg, kg, cw, av, dtb, eb, eg, og) = lw
    b, t, _ = x.shape
    x2d = x.reshape(b * t, D_MODEL)
    prompt = cache_k is None
    outs = _inproj(x2d, g1, w_cat, bd, qg, kg, seq_len=t, attn_layout=prompt)
    qkva, z, bg, qb, kb, vb = outs[:6]
    r3 = lambda a: a.reshape(b, t, a.shape[-1])
    qkva3 = r3(qkva)
    c = min(DELTA_CHUNK, t)
    tt = min(ROW_TILE, t)
    o_a, s_fin = _gdn(qkva3, r3(z), r3(bg), cbuf, _blockdiag_state(s0), cw, av, dtb, eb, eg, bd, og,
                      tt=tt, c=c)
    new_conv = jnp.concatenate([cbuf, qkva3], axis=1)[:, t:] if t < CONV_WIDTH - 1 else qkva3[:, t - (CONV_WIDTH - 1):]
    qb3 = r3(qb)
    rb = rel_bias.astype(F32)

    if prompt:
        kbf, vt, km = outs[6:]
        heads_last = lambda a: jnp.transpose(a.reshape(b, N_HEADS, HEAD_DIM, t), (0, 3, 1, 2))
        k4, v4 = heads_last(kb), heads_last(vb)
        nb = t // MOBA_BLOCK
        kmp = jnp.pad(km.reshape(b, nb, WIDTH), ((0, 0), (0, LANES - nb), (0, 0)))
        ii = np.arange(MOBA_BLOCK)
        d_own = ii[None, :] - ii[:, None]
        far_b = _bucket_np(np.arange(MOBA_BLOCK + 1, max(t, MOBA_BLOCK + 2)))
        assert (far_b == far_b[0]).all()
        bias_tab = _bias_table(rel_bias, np.concatenate([d_own, d_own + MOBA_BLOCK], axis=0))
        bias_tab = bias_tab.reshape(N_HEADS, 2, MOBA_BLOCK, MOBA_BLOCK)
        far = rb[int(far_b[0])]
        o_b = _moba_prompt(qb3, r3(kbf), vt, kmp, bias_tab, far).reshape(b * t, WIDTH)
    else:
        npages = page_table.shape[1]
        page = cache_k.shape[1]
        past = npages * page
        assert page == LANES and past % MOBA_BLOCK == 0 and t <= LANES
        n_past = past // MOBA_BLOCK
        pps = next(n for n in (32, 16, 8, 4, 2) if npages % n == 0)
        rows = N_HEADS * t
        pt_flat = page_table.reshape(-1).astype(jnp.int32)
        slab = lambda cch: jnp.transpose(cch, (0, 2, 3, 1)).reshape(cch.shape[0], WIDTH, page)
        kb3, vb3 = r3(kb), r3(vb)
        k4, v4 = (a.reshape(b, t, N_HEADS, HEAD_DIM) for a in (kb3, vb3))
        scores, kmt = _dec_scores(pt_flat, qb3, slab(cache_k), npages=npages, pps=pps)
        tpos = np.arange(t)
        d_past = past + tpos[:, None] - np.arange(past)[None, :]
        bias_past = _bias_table(rel_bias, d_past).reshape(rows, past)
        d_own = np.zeros((t, LANES), np.int64)
        d_own[:, :t] = tpos[:, None] - tpos[None, :]
        bias_own = _bias_table(rel_bias, d_own).reshape(rows, LANES)
        ex = np.zeros((LANES, past), np.float32)
        ex[np.arange(past) // MOBA_BLOCK, np.arange(past)] = 1.0
        probs, o_own = _dec_select(scores, kmt, qb3, kb3, vb3, bias_past, bias_own, jnp.asarray(ex, BF16),
                                   n_past=n_past)
        o_b = _dec_pv(pt_flat, probs, o_own, slab(cache_v), npages=npages, pps=pps, tq=t).reshape(b * t, WIDTH)
    marker = None if prompt else scores[0, :SUBLANES, :LANES]
    return o_a.reshape(b * t, WIDTH), o_b, k4, v4, _unblock_state(s_fin), new_conv, marker


def _moe_ffn(xn, gates, er, counts, x1, rows, after, wg, bg, wu, bu, wd, bdn):
    n = xn.shape[0]
    nk = n * TOP_K
    flat_e = er[:, :TOP_K].reshape(nk)
    rank = er[:, TOP_K:2 * TOP_K].reshape(nk)
    counts = counts[0, :N_EXPERTS].astype(jnp.int32)
    padded = ((counts + rows - 1) // rows) * rows
    pad_end = jnp.cumsum(padded)
    pad_start = pad_end - padded
    dest = pad_start[flat_e] + rank
    n_blocks = -(-nk // rows) + N_EXPERTS
    n_rows = n_blocks * rows
    flat_tok = jnp.arange(nk, dtype=jnp.int32) // TOP_K
    slot_tok = jnp.zeros((n_rows,), jnp.int32).at[dest].set(flat_tok, unique_indices=True)
    nvalid = (pad_end[-1] // rows).astype(jnp.int32).reshape(1)
    blk_start = jnp.arange(n_blocks, dtype=jnp.int32) * rows
    block_e = jnp.sum((blk_start[:, None] >= pad_end[None, :]).astype(jnp.int32), axis=1)
    block_e = jnp.minimum(block_e, N_EXPERTS - 1)
    last_e = block_e[jnp.maximum(nvalid[0] - 1, 0)]
    block_e = jnp.where(jnp.arange(n_blocks) < nvalid[0], block_e, last_e)
    yb = _moe(block_e, nvalid, xn[slot_tok], wg, bg, wu, bu, wd, bdn, after, rows=rows)
    dest2 = dest.reshape(n, TOP_K)
    y = x1
    for kk in range(TOP_K):
        y = y + gates[:, kk:kk + 1] * yb[dest2[:, kk]]
    return y


def kernel(x_prompt, x_sample, cache_k, cache_v, state_delta, state_conv, page_table, rel_bias,
           norm1_g, w_in, conv_w, A_log, dt_bias, o_norm_g, q_norm_g, k_norm_g, w_out, norm2_g,
           w_router, b_router, w_gate, b_gate, w_up, b_up, w_down, b_down):
    depth = norm1_g.shape[0]
    bp, tp, _ = x_prompt.shape
    bs, ts, _ = x_sample.shape
    yp, ys = x_prompt, x_sample
    outs = [[] for _ in range(8)]
    seg = np.arange(WIDTH) // HEAD_DIM
    segw = np.arange(GW) // HEAD_DIM
    bd = jnp.asarray((segw[:, None] == segw[None, :]).astype(np.float32) / HEAD_DIM, BF16)
    eb_np = np.zeros((LANES, WIDTH), np.float32)
    eb_np[seg, np.arange(WIDTH)] = 1.0
    eg_np = np.zeros((LANES, WIDTH), np.float32)
    eg_np[N_HEADS + seg, np.arange(WIDTH)] = 1.0
    eb, eg = jnp.asarray(eb_np, BF16), jnp.asarray(eg_np, BF16)
    c1, c3 = QKV + WIDTH, QKV + WIDTH + 2 * N_HEADS
    for l in range(depth):
        wl = w_in[l]
        w_cat = jnp.concatenate([wl[:, :c1], wl[:, c3:], wl[:, c1:c3],
                                 jnp.zeros((D_MODEL, LANES - 2 * N_HEADS), wl.dtype)], axis=1).astype(BF16)
        tile8 = lambda g: jnp.tile(g.astype(F32), N_HEADS).reshape(1, WIDTH)
        av = jnp.zeros((1, LANES), F32).at[0, N_HEADS:2 * N_HEADS].set(-jnp.exp(A_log[l].astype(F32)))
        dtb = jnp.zeros((1, LANES), F32).at[0, N_HEADS:2 * N_HEADS].set(dt_bias[l].astype(F32))
        lw = (norm1_g[l].astype(F32).reshape(1, D_MODEL), w_cat, bd, tile8(q_norm_g[l]), tile8(k_norm_g[l]),
              conv_w[l].astype(F32), av, dtb, eb, eg, tile8(o_norm_g[l]))
        w_o = w_out[l].astype(BF16)
        g2 = norm2_g[l].astype(F32).reshape(1, D_MODEL)
        wr = jnp.pad(w_router[l].astype(F32), ((0, 0), (0, LANES - N_EXPERTS)))
        br = jnp.pad(b_router[l].astype(F32), (0, LANES - N_EXPERTS)).reshape(1, LANES)
        cnt0 = jnp.zeros((1, LANES), F32)
        ffn_w = (w_gate[l].astype(F32), b_gate[l].astype(F32)[:, None, :],
                 w_up[l].astype(F32), b_up[l].astype(F32)[:, None, :],
                 w_down[l].astype(F32), b_down[l].astype(F32)[:, None, :])

        def ffn(oa, ob, x, after):
            n = x.shape[0] * x.shape[1]
            x1, xn, gt, er, cnt = _outproj(oa, ob, x.reshape(n, D_MODEL), w_o, g2, wr, br, cnt0)
            rows = MOE_ROWS if n * TOP_K >= 2 * N_EXPERTS * MOE_ROWS else MOE_ROWS_SMALL
            return _moe_ffn(xn, gt, er, cnt, x1, rows, after, *ffn_w).reshape(x.shape)

        s0_p = jnp.zeros((bp, N_HEADS, HEAD_DIM, HEAD_DIM), F32)
        c0_p = jnp.zeros((bp, CONV_WIDTH - 1, QKV), F32)
        oa_p, ob_p, k_p, v_p, s_p, c_p, _ = _mixer(yp, None, None, None, s0_p, c0_p, rel_bias, lw)
        oa_s, ob_s, k_s, v_s, s_s, c_s, dec_marker = _mixer(ys, cache_k[l], cache_v[l], page_table,
                                                            state_delta[l], state_conv[l], rel_bias, lw)
        yp = ffn(oa_p, ob_p, yp, dec_marker)
        ys = ffn(oa_s, ob_s, ys, jnp.zeros((SUBLANES, LANES), F32))
        for lst, val in zip(outs, (k_p, v_p, s_p, c_p, k_s, v_s, s_s, c_s)):
            lst.append(val)
    stacked = [jnp.stack(o) for o in outs]
    return (yp, ys, *stacked)
```

```python
import functools
import math

import numpy as np
import jax
import jax.numpy as jnp
from jax import lax
from jax.experimental import pallas as pl
from jax.experimental.pallas import tpu as pltpu

F32 = jnp.float32
BF16 = jnp.bfloat16

D_MODEL = 1024
N_HEADS = 8
HEAD_DIM = 64
WIDTH = N_HEADS * HEAD_DIM
QKV = 3 * WIDTH
CONV_WIDTH = 4
DELTA_CHUNK = 64
MOBA_BLOCK = 256
MOBA_TOP = 3
ATTN_SCALE = HEAD_DIM ** -0.5
NUM_BUCKETS = 32
MAX_DISTANCE = 128
N_EXPERTS = 32
TOP_K = 4
SWIGLU_LIMIT = 7.0
SWIGLU_ALPHA = 1.702
RMS_EPS = 1e-6

LANES = 128
SUBLANES = 8
ROW_TILE = 256
GDN_TILE = 256
MOE_ROWS = 512
MOE_ROWS_SMALL = 128
NEG = -1e30
VMEM_LIMIT = 48 * 1024 * 1024

GROUP = 4
GW = GROUP * HEAD_DIM
NPAIR = WIDTH // LANES
FAR_BLOCKS = 4
KEY_SUB = 128
SCORE_LOOKAHEAD = 3

_NN = (((1,), (0,)), ((), ()))
_NT = (((1,), (1,)), ((), ()))
_TN = (((0,), (0,)), ((), ()))


def _dot(a, b, dims=_NN):
    return lax.dot_general(a, b, dims, preferred_element_type=F32)


def _split2(a):
    hi = a.astype(BF16)
    lo = (a - hi.astype(F32)).astype(BF16)
    return hi, lo


def _split3(a):
    hi = a.astype(BF16)
    r = a - hi.astype(F32)
    mid = r.astype(BF16)
    lo = (r - mid.astype(F32)).astype(BF16)
    return hi, mid, lo


def _mm1(a, b, dims=_NN):
    return _dot(a.astype(BF16), b.astype(BF16), dims)


def _mm3(a, b, dims=_NN):
    ah, al = _split2(a)
    bh, bl = _split2(b)
    return _dot(ah, bh, dims) + (_dot(ah, bl, dims) + _dot(al, bh, dims))


def _mm_exact_rhs(a, b_bf16, dims=_NN):
    hi, mid, lo = _split3(a)
    return _dot(hi, b_bf16, dims) + (_dot(mid, b_bf16, dims) + _dot(lo, b_bf16, dims))


def _mm_exact_lhs(a_bf16, b, dims=_NN):
    hi, mid, lo = _split3(b)
    return _dot(a_bf16, hi, dims) + (_dot(a_bf16, mid, dims) + _dot(a_bf16, lo, dims))


def _seg_mean(x, bd):
    outs = []
    for g in range(x.shape[1] // GW):
        hi, lo = _split2(x[:, g * GW:(g + 1) * GW])
        outs.append(_dot(hi, bd) + _dot(lo, bd))
    return outs[0] if len(outs) == 1 else jnp.concatenate(outs, axis=1)


def _sigmoid(x):
    return 1.0 / (1.0 + jnp.exp(-x))


def _softplus(x):
    return jnp.maximum(x, 0.0) + jnp.log(1.0 + jnp.exp(-jnp.abs(x)))


def _inproj_body(x_ref, g_ref, w_ref, bd_ref, qg_ref, kg_ref,
                 qkva_ref, z_ref, bg_ref, qb_ref, kb_ref, vb_ref, *attn_refs):
    x = x_ref[...]
    ms = jnp.mean(x * x, axis=-1, keepdims=True)
    h = (x * lax.rsqrt(ms + RMS_EPS) * g_ref[...]).astype(BF16)

    def proj(lo, hi):
        return _dot(h, w_ref[:, lo:hi])

    bd = bd_ref[...]
    qkva_ref[...] = proj(0, QKV)
    z_ref[...] = proj(QKV, QKV + WIDTH)
    c = QKV + WIDTH
    q = proj(c, c + WIDTH)
    qb_ref[...] = q * lax.rsqrt(_seg_mean(q * q, bd) + RMS_EPS) * qg_ref[...]
    k = proj(c + WIDTH, c + 2 * WIDTH)
    kn = k * lax.rsqrt(_seg_mean(k * k, bd) + RMS_EPS) * kg_ref[...]
    v = proj(c + 2 * WIDTH, c + 3 * WIDTH)
    bg_ref[...] = proj(c + 3 * WIDTH, c + 3 * WIDTH + LANES)
    if attn_refs:
        kbf_ref, vt_ref, km_ref = attn_refs
        kb_ref[0] = kn.T
        vt = v.T
        vb_ref[0] = vt
        kbf_ref[...] = kn.astype(BF16)
        km_ref[0] = jnp.mean(kn, axis=0, keepdims=True)
        vtb = vt.astype(BF16)
        for pp in range(NPAIR):
            vt_ref[0, pp, 0] = vtb[pp * LANES:(pp + 1) * LANES, :]
    else:
        kb_ref[...] = kn
        vb_ref[...] = v


def _inproj(x2d, g1, w_cat, bd, qg, kg, *, seq_len, attn_layout):
    n = x2d.shape[0]
    tm = ROW_TILE
    nt = n // tm
    row = lambda w: pl.BlockSpec((tm, w), lambda i: (i, 0))
    full = lambda a: pl.BlockSpec(a.shape, lambda i: (0,) * a.ndim)
    out_shape = [
        jax.ShapeDtypeStruct((n, QKV), F32), jax.ShapeDtypeStruct((n, WIDTH), F32),
        jax.ShapeDtypeStruct((n, LANES), F32), jax.ShapeDtypeStruct((n, WIDTH), F32),
    ]
    out_specs = [row(QKV), row(WIDTH), row(LANES), row(WIDTH)]
    if not attn_layout:
        out_shape += [jax.ShapeDtypeStruct((n, WIDTH), F32)] * 2
        out_specs += [row(WIDTH)] * 2
    else:
        tpb = seq_len // tm
        kv_t = jax.ShapeDtypeStruct((n // seq_len, WIDTH, seq_len), F32)
        kv_spec = pl.BlockSpec((1, WIDTH, tm), lambda i: (i // tpb, 0, i % tpb))
        out_shape += [kv_t, kv_t]
        out_specs += [kv_spec, kv_spec]
        out_shape += [jax.ShapeDtypeStruct((n, WIDTH), BF16),
                      jax.ShapeDtypeStruct((n // seq_len, NPAIR, tpb, LANES, tm), BF16),
                      jax.ShapeDtypeStruct((nt, 1, WIDTH), F32)]
        out_specs += [row(WIDTH),
                      pl.BlockSpec((1, NPAIR, 1, LANES, tm), lambda i: (i // tpb, 0, i % tpb, 0, 0)),
                      pl.BlockSpec((1, 1, WIDTH), lambda i: (i, 0, 0))]
    return pl.pallas_call(
        _inproj_body,
        grid=(nt,),
        in_specs=[row(D_MODEL), full(g1), full(w_cat), full(bd), full(qg), full(kg)],
        out_specs=tuple(out_specs),
        out_shape=tuple(out_shape),
        compiler_params=pltpu.CompilerParams(dimension_semantics=("parallel",),
                                             vmem_limit_bytes=VMEM_LIMIT),
        name="inproj",
    )(x2d, g1, w_cat, bd, qg, kg)


def _stack(x, masks):
    return jnp.concatenate([jnp.where(m, x, 0.0) for m in masks], axis=0)


def _unstack(w, c, n):
    acc = w[0:c]
    for h in range(1, n):
        acc = acc + w[h * c:(h + 1) * c]
    return acc


def _gdn_body(qkva_ref, z_ref, bg_ref, cbuf_ref, s0_ref, cw_ref, av_ref, dtb_ref, eb_ref, eg_ref,
              bd_ref, og_ref, o_ref, sfin_ref, s_scr, carry_scr, uext_scr, *, tt, c):
    t = pl.program_id(1)
    nt = pl.num_programs(1)
    gc = GROUP * c

    @pl.when(t == 0)
    def _():
        s_scr[...] = s0_ref[0]
        carry_scr[...] = jnp.zeros_like(carry_scr)
        carry_scr[SUBLANES - (CONV_WIDTH - 1):SUBLANES, :] = cbuf_ref[0]

    uext_scr[0:SUBLANES, :] = carry_scr[...]
    uext_scr[SUBLANES:SUBLANES + tt, :] = qkva_ref[0]
    carry_scr[...] = uext_scr[tt:tt + SUBLANES, :]

    cw = cw_ref[...]
    off = SUBLANES - (CONV_WIDTH - 1)
    conv = cw[0:1] * uext_scr[off:off + tt, :]
    for i in range(1, CONV_WIDTH):
        conv = conv + cw[i:i + 1] * uext_scr[off + i:off + i + tt, :]
    act = conv * _sigmoid(conv)

    bd = bd_ref[...]
    q = act[:, 0:WIDTH]
    k = act[:, WIDTH:2 * WIDTH]
    v = act[:, 2 * WIDTH:3 * WIDTH]
    q = q * lax.rsqrt(_seg_mean(q * q, bd) * HEAD_DIM + RMS_EPS) * (HEAD_DIM ** -0.5)
    k = k * lax.rsqrt(_seg_mean(k * k, bd) * HEAD_DIM + RMS_EPS)

    bg = bg_ref[0]
    beta = _mm_exact_rhs(_sigmoid(bg), eb_ref[...])
    gx = _mm_exact_rhs(av_ref[...] * _softplus(bg + dtb_ref[...]), eg_ref[...])

    r4 = lax.broadcasted_iota(jnp.int32, (gc, gc), 0)
    c4 = lax.broadcasted_iota(jnp.int32, (gc, gc), 1)
    same = (r4 // c) == (c4 // c)
    tri = jnp.logical_and(same, r4 >= c4)
    strict = jnp.logical_and(same, r4 > c4)
    eye = (r4 == c4).astype(F32)
    rc = lax.broadcasted_iota(jnp.int32, (c, c), 0)
    cc = lax.broadcasted_iota(jnp.int32, (c, c), 1)
    tril_c = (rc >= cc).astype(BF16)
    lane_g = lax.broadcasted_iota(jnp.int32, (1, GW), 1) // HEAD_DIM
    hmasks = [lane_g == h for h in range(GROUP)]
    rs = lax.broadcasted_iota(jnp.int32, (GW, GW), 0) // HEAD_DIM
    cs = lax.broadcasted_iota(jnp.int32, (GW, GW), 1) // HEAD_DIM
    bdm = (rs == cs).astype(F32)
    nsq = int(round(math.log2(c))) - 1

    nch, ngr = tt // c, N_HEADS // GROUP
    inst = []
    for ch in range(nch):
        r0 = ch * c
        gcum = _mm_exact_lhs(tril_c, gx[r0:r0 + c])
        eg = jnp.exp(gcum)
        glast = gcum[c - 1:c, :]
        kscale = jnp.exp(glast - gcum)
        gtot = jnp.exp(glast)
        for gr in range(ngr):
            sl = slice(gr * GW, (gr + 1) * GW)
            kq = k[r0:r0 + c, sl]
            qq = q[r0:r0 + c, sl]
            bb = beta[r0:r0 + c, sl]
            gq = gcum[:, sl]
            kb = kq * bb
            xk = _stack(kq, hmasks).astype(BF16)
            kk = _dot(_stack(kb, hmasks).astype(BF16), xk, _NT)
            qk = _dot(_stack(qq, hmasks).astype(BF16), xk, _NT)
            gcol = jnp.concatenate(
                [jnp.broadcast_to(gq[:, h * HEAD_DIM:h * HEAD_DIM + 1], (c, gc)) for h in range(GROUP)], axis=0)
            if gc % LANES == 0:
                grow = gcol.T
            else:
                grow = _mm_exact_lhs(jnp.full((gc, GW), 1.0 / HEAD_DIM, BF16), _stack(gq, hmasks), _NT)
            dm = jnp.where(tri, jnp.exp(jnp.where(tri, gcol - grow, 0.0)), 0.0)
            lmat = jnp.where(strict, kk * dm, 0.0)
            rhs = jnp.concatenate([_stack(v[r0:r0 + c, sl] * bb, hmasks), _stack(kb * eg[:, sl], hmasks)], axis=1)
            inst.append(dict(gr=gr, sl=sl, amat=qk * dm, pinv=eye - lmat, msq=lmat, rhs=rhs,
                             qe=qq * eg[:, sl], kd=kq * kscale[:, sl], gtot=gtot[:, sl]))
    for _ in range(nsq):
        for it in inst:
            it["msq"] = _mm1(it["msq"], it["msq"])
        for it in inst:
            it["pinv"] = it["pinv"] + _mm1(it["pinv"], it["msq"])
    for it in inst:
        w = _mm1(it["pinv"], it["rhs"])
        it["value"] = _unstack(w[:, 0:GW], c, GROUP)
        it["kcum"] = _unstack(w[:, GW:2 * GW], c, GROUP)
    o_rows = []
    for ch in range(nch):
        o_groups = []
        for it in inst[ch * ngr:(ch + 1) * ngr]:
            s = s_scr[it["gr"]]
            u = it["value"] - _mm3(it["kcum"], s)
            o_groups.append(_mm1(it["qe"], s) + _unstack(_mm1(it["amat"], _stack(u, hmasks)), c, GROUP))
            s_scr[it["gr"]] = s * it["gtot"] + bdm * _mm3(it["kd"], u, _TN)
        o_rows.append(jnp.concatenate(o_groups, axis=1))
    o = o_rows[0] if len(o_rows) == 1 else jnp.concatenate(o_rows, axis=0)
    on = o * lax.rsqrt(_seg_mean(o * o, bd) + RMS_EPS) * og_ref[...]
    z = z_ref[0]
    o_ref[0] = on * (z * _sigmoid(z))

    @pl.when(t == nt - 1)
    def _():
        sfin_ref[0] = s_scr[...]


def _gdn(qkva, z, bg, cbuf, s0bd, cw, av, dtb, eb, eg, bd, og, *, tt, c):
    b, t, _ = qkva.shape
    ng = N_HEADS // GROUP
    full = lambda a: pl.BlockSpec(a.shape, lambda i, j: (0,) * a.ndim)
    body = functools.partial(_gdn_body, tt=tt, c=c)
    return pl.pallas_call(
        body,
        grid=(b, t // tt),
        in_specs=[
            pl.BlockSpec((1, tt, QKV), lambda i, j: (i, j, 0)),
            pl.BlockSpec((1, tt, WIDTH), lambda i, j: (i, j, 0)),
            pl.BlockSpec((1, tt, LANES), lambda i, j: (i, j, 0)),
            pl.BlockSpec((1, CONV_WIDTH - 1, QKV), lambda i, j: (i, 0, 0)),
            pl.BlockSpec((1, ng, GW, GW), lambda i, j: (i, 0, 0, 0)),
            full(cw), full(av), full(dtb), full(eb), full(eg), full(bd), full(og),
        ],
        out_specs=(pl.BlockSpec((1, tt, WIDTH), lambda i, j: (i, j, 0)),
                   pl.BlockSpec((1, ng, GW, GW), lambda i, j: (i, 0, 0, 0))),
        out_shape=(jax.ShapeDtypeStruct((b, t, WIDTH), F32),
                   jax.ShapeDtypeStruct((b, ng, GW, GW), F32)),
        scratch_shapes=[pltpu.VMEM((ng, GW, GW), F32), pltpu.VMEM((SUBLANES, QKV), F32),
                        pltpu.VMEM((tt + SUBLANES, QKV), F32)],
        compiler_params=pltpu.CompilerParams(dimension_semantics=("arbitrary", "arbitrary"),
                                             vmem_limit_bytes=VMEM_LIMIT),
        name="gdn",
    )(qkva, z, bg, cbuf, s0bd, cw, av, dtb, eb, eg, bd, og)


def _bucket_np(dist):
    n = np.maximum(dist, 0)
    max_exact = NUM_BUCKETS // 2
    n_f = np.maximum(n, max_exact).astype(np.float32)
    large = max_exact + (np.log(n_f / np.float32(max_exact)) / np.float32(math.log(MAX_DISTANCE / max_exact))
                         * np.float32(NUM_BUCKETS - max_exact)).astype(np.int32)
    large = np.minimum(large, NUM_BUCKETS - 1)
    return np.where(n < max_exact, n, large).astype(np.int32)


def _bucket_steps(max_dist):
    b = _bucket_np(np.arange(max_dist + 1))
    assert (np.diff(b) >= 0).all()
    firsts = np.flatnonzero(np.diff(b, prepend=-1))
    return [(int(d), int(b[d])) for d in firsts]


def _bias_body(rb_ref, d_ref, o_ref, *, steps):
    h = pl.program_id(0)
    d = d_ref[...]
    acc = jnp.full(d.shape, rb_ref[steps[0][1] * N_HEADS + h], F32)
    for first, bucket in steps[1:]:
        acc = jnp.where(d >= first, rb_ref[bucket * N_HEADS + h], acc)
    o_ref[0] = acc


def _bias_table(rel_bias, dist_np):
    r, c = dist_np.shape
    steps = _bucket_steps(int(max(dist_np.max(), 0)))
    grid_spec = pltpu.PrefetchScalarGridSpec(
        num_scalar_prefetch=1,
        grid=(N_HEADS,),
        in_specs=[pl.BlockSpec((r, c), lambda h, rb: (0, 0))],
        out_specs=pl.BlockSpec((1, r, c), lambda h, rb: (h, 0, 0)),
    )
    return pl.pallas_call(
        functools.partial(_bias_body, steps=steps),
        grid_spec=grid_spec,
        out_shape=jax.ShapeDtypeStruct((N_HEADS, r, c), F32),
        compiler_params=pltpu.CompilerParams(dimension_semantics=("arbitrary",),
                                             vmem_limit_bytes=VMEM_LIMIT),
        name="bias_table",
    )(rel_bias.astype(F32).reshape(-1), jnp.asarray(np.maximum(dist_np, 0), jnp.int32))


def _top_select(gate, idx, n_cand, n_valid_f, axis):
    g = jnp.where(idx < n_cand, gate, -jnp.inf)
    sel = jnp.zeros(gate.shape, F32)
    for r in range(MOBA_TOP):
        mx = jnp.max(g, axis=axis, keepdims=True)
        first = jnp.min(jnp.where(g == mx, idx, jnp.int32(1 << 30)), axis=axis, keepdims=True)
        hit = idx == first
        sel = jnp.maximum(sel, jnp.where(hit, n_valid_f[r], 0.0))
        g = jnp.where(hit, -jnp.inf, g)
    return sel


def _moba_prompt_body(far_ref, q_ref, k_ref, vt_ref, km_ref, bias_ref, o_ref, sel_scr):
    p = pl.program_id(1)
    i = pl.program_id(2)
    blk = MOBA_BLOCK
    qt = (q_ref[0] * ATTN_SCALE).T
    row = lax.broadcasted_iota(jnp.int32, (LANES, blk), 0)
    top = row < HEAD_DIM
    qt_m = [jnp.where(top, qt, 0.0), jnp.where(top, 0.0, qt)]
    qtb = [x.astype(BF16) for x in qt_m]
    km = km_ref[0]
    n_valid = [(i > r).astype(F32) for r in range(MOBA_TOP)]
    for hh in range(2):
        sel_scr[hh] = _top_select(_mm3(km, qt_m[hh]), row, i, n_valid, 0)
    sub = KEY_SUB

    def step(n, carry, nblk):
        ms, ls, accs = [list(x) for x in carry]

        def qk(j, u, hh):
            start = pl.multiple_of((n + j) * blk + u * sub, sub)
            return _dot(k_ref[0, pl.ds(start, sub), :], qtb[hh])

        def absorb(j, u, hh, s):
            shift = far_ref[2 * p + hh]
            valid = sel_scr[hh, pl.ds(n + j, 1), :] > 0.5
            bm = jnp.where(valid, jnp.max(s, axis=0, keepdims=True) + shift, NEG)
            m_new = jnp.maximum(ms[hh], bm)
            alpha = jnp.exp(ms[hh] - m_new)
            off = jnp.where(valid, m_new - shift, -NEG)
            pr = jnp.exp(s - off)
            ls[hh] = alpha * ls[hh] + jnp.sum(pr, axis=0, keepdims=True)
            ms[hh] = m_new
            vh = vt_ref[0, 0, n + j, hh * HEAD_DIM:(hh + 1) * HEAD_DIM, u * sub:(u + 1) * sub]
            accs[hh] = accs[hh] * alpha + _dot(vh, pr.astype(BF16))

        tiles = [(j, u, hh) for j in range(nblk) for u in range(blk // sub) for hh in range(2)]
        pending = []
        for tile in tiles:
            pending.append(tile + (qk(*tile),))
            if len(pending) > SCORE_LOOKAHEAD:
                absorb(*pending.pop(0))
        for item in pending:
            absorb(*item)
        return tuple(ms), tuple(ls), tuple(accs)

    init = ((jnp.full((1, blk), NEG, F32),) * 2, (jnp.zeros((1, blk), F32),) * 2,
            (jnp.zeros((HEAD_DIM, blk), F32),) * 2)
    n_far = jnp.maximum(i - 1, 0)
    n_wide = n_far // FAR_BLOCKS
    carry = lax.fori_loop(0, n_wide, lambda w, cr: step(w * FAR_BLOCKS, cr, FAR_BLOCKS), init)
    done = n_wide * FAR_BLOCKS
    width = FAR_BLOCKS // 2
    while width >= 1:
        take = ((n_far - done) >= width).astype(jnp.int32)
        carry = lax.fori_loop(0, take, lambda _, cr, d=done, w=width: step(d, cr, w), carry)
        done = done + take * width
        width //= 2
    rowb = lax.broadcasted_iota(jnp.int32, (blk, blk), 0)
    colb = lax.broadcasted_iota(jnp.int32, (blk, blk), 1)

    def last_step(n, carry, modes):
        ms, ls, accs = [list(x) for x in carry]
        for hh in range(2):
            ss, valids, bmax = [], [], []
            for j, mode in enumerate(modes):
                s = _dot(k_ref[0, pl.ds(pl.multiple_of((n + j) * blk, blk), blk), :], qtb[hh])
                if mode == "own":
                    s = jnp.where(rowb <= colb, s + bias_ref[hh, 0], NEG)
                    valid = None
                    bm = jnp.max(s, axis=0, keepdims=True)
                else:
                    s = s + bias_ref[hh, 1]
                    valid = sel_scr[hh, pl.ds(n + j, 1), :] > 0.5
                    bm = jnp.where(valid, jnp.max(s, axis=0, keepdims=True), NEG)
                ss.append(s)
                valids.append(valid)
                bmax.append(bm)
            m_new = ms[hh]
            for bm in bmax:
                m_new = jnp.maximum(m_new, bm)
            alpha = jnp.exp(ms[hh] - m_new)
            lsum = alpha * ls[hh]
            acc = accs[hh] * alpha
            for j, s in enumerate(ss):
                off = m_new if valids[j] is None else jnp.where(valids[j], m_new, -NEG)
                pr = jnp.exp(s - off)
                lsum = lsum + jnp.sum(pr, axis=0, keepdims=True)
                acc = acc + _dot(vt_ref[0, 0, n + j, hh * HEAD_DIM:(hh + 1) * HEAD_DIM, :], pr.astype(BF16))
            ls[hh], ms[hh], accs[hh] = lsum, m_new, acc
        return tuple(ms), tuple(ls), tuple(accs)

    has_prev = (i >= 1).astype(jnp.int32)
    carry = lax.fori_loop(0, has_prev, lambda _, cr: last_step(i - 1, cr, ("prev", "own")), carry)
    ms, ls, accs = lax.fori_loop(0, 1 - has_prev, lambda _, cr: last_step(i, cr, ("own",)), carry)
    o_ref[0] = jnp.concatenate([accs[0] / ls[0], accs[1] / ls[1]], axis=0).T


def _moba_prompt(qb, kbf, vt, kmp, bias_tab, far):
    b, t, _ = qb.shape
    nq = t // MOBA_BLOCK
    grid_spec = pltpu.PrefetchScalarGridSpec(
        num_scalar_prefetch=1,
        grid=(b, NPAIR, nq),
        in_specs=[
            pl.BlockSpec((1, MOBA_BLOCK, LANES), lambda bi, p, i, far: (bi, i, p)),
            pl.BlockSpec((1, t, LANES), lambda bi, p, i, far: (bi, 0, p)),
            pl.BlockSpec((1, 1, nq, LANES, MOBA_BLOCK), lambda bi, p, i, far: (bi, p, 0, 0, 0)),
            pl.BlockSpec((1, LANES, LANES), lambda bi, p, i, far: (bi, 0, p)),
            pl.BlockSpec((2, 2, MOBA_BLOCK, MOBA_BLOCK), lambda bi, p, i, far: (p, 0, 0, 0)),
        ],
        out_specs=pl.BlockSpec((1, MOBA_BLOCK, LANES), lambda bi, p, i, far: (bi, i, p)),
        scratch_shapes=[pltpu.VMEM((2, LANES, MOBA_BLOCK), F32)],
    )
    return pl.pallas_call(
        _moba_prompt_body,
        grid_spec=grid_spec,
        out_shape=jax.ShapeDtypeStruct((b, t, WIDTH), F32),
        compiler_params=pltpu.CompilerParams(
            dimension_semantics=("arbitrary", "arbitrary", "arbitrary"),
            vmem_limit_bytes=VMEM_LIMIT),
        name="moba_prompt",
    )(far, qb, kbf, vt, kmp, bias_tab)


def _head_masks(width):
    lane_h = lax.broadcasted_iota(jnp.int32, (1, width), 1) // HEAD_DIM
    return [lane_h == h for h in range(width // HEAD_DIM)]


def _dec_scores_body(pt_ref, q_ref, *refs, pps):
    k_refs = refs[:pps]
    s_ref, kmt_ref = refs[pps], refs[pps + 1]
    masks = _head_masks(WIDTH)
    qs = _stack(q_ref[0] * ATTN_SCALE, masks).astype(BF16)
    ppb = MOBA_BLOCK // LANES
    j = pl.program_id(1)
    for i in range(pps):
        s_ref[0, :, i * LANES:(i + 1) * LANES] = _dot(qs, k_refs[i][0].astype(BF16))
    lane = lax.broadcasted_iota(jnp.int32, (1, LANES), 1)
    @pl.when(j == 0)
    def _():
        kmt_ref[0] = jnp.zeros((WIDTH, LANES), F32)

    kmt = kmt_ref[0]
    for blk in range(pps // ppb):
        tot = k_refs[blk * ppb][0]
        for jj in range(1, ppb):
            tot = tot + k_refs[blk * ppb + jj][0]
        mean = jnp.sum(tot, axis=1, keepdims=True) * (1.0 / MOBA_BLOCK)
        kmt = jnp.where(lane == j * (pps // ppb) + blk, mean, kmt)
    kmt_ref[0] = kmt


def _page_specs(pps, npages):
    def mk(i):
        return pl.BlockSpec((1, WIDTH, LANES), lambda b, j, pt: (pt[b * npages + j * pps + i], 0, 0))
    return [mk(i) for i in range(pps)]


def _dec_scores(pt_flat, q, cache, *, npages, pps):
    b, tq, _ = q.shape
    ppb = MOBA_BLOCK // LANES
    nj = npages // pps
    grid_spec = pltpu.PrefetchScalarGridSpec(
        num_scalar_prefetch=1,
        grid=(b, nj),
        in_specs=[pl.BlockSpec((1, tq, WIDTH), lambda bi, j, pt: (bi, 0, 0))] + _page_specs(pps, npages),
        out_specs=(pl.BlockSpec((1, N_HEADS * tq, pps * LANES), lambda bi, j, pt: (bi, 0, j)),
                   pl.BlockSpec((1, WIDTH, LANES), lambda bi, j, pt: (bi, 0, 0))),
    )
    assert npages // ppb <= LANES
    return pl.pallas_call(
        functools.partial(_dec_scores_body, pps=pps),
        grid_spec=grid_spec,
        out_shape=(jax.ShapeDtypeStruct((b, N_HEADS * tq, npages * LANES), F32),
                   jax.ShapeDtypeStruct((b, WIDTH, LANES), F32)),
        compiler_params=pltpu.CompilerParams(dimension_semantics=("arbitrary", "arbitrary"),
                                             vmem_limit_bytes=VMEM_LIMIT),
        name="dec_scores",
    )(pt_flat, q, *([cache] * pps))


def _dec_select_body(s_ref, km_ref, q_ref, kn_ref, vn_ref, bp_ref, bo_ref, ex_ref, p_ref, oo_ref,
                     *, tq, n_past):
    rows = N_HEADS * tq
    masks = _head_masks(WIDTH)
    qs = _stack(q_ref[0] * ATTN_SCALE, masks)
    col = lax.broadcasted_iota(jnp.int32, (rows, LANES), 1)
    n_valid = [jnp.float32(1.0 if n_past > r else 0.0) for r in range(MOBA_TOP)]
    sel = _top_select(_mm3(qs, km_ref[0]), col, n_past, n_valid, -1)
    selk = _dot(sel.astype(BF16), ex_ref[...])
    logit = jnp.where(selk > 0.5, s_ref[0] + bp_ref[...], NEG)
    pad = jnp.zeros((LANES - tq, WIDTH), F32)
    kn = jnp.concatenate([kn_ref[0], pad], axis=0).astype(BF16)
    vn = jnp.concatenate([vn_ref[0], pad], axis=0).astype(BF16)
    trow = lax.rem(lax.broadcasted_iota(jnp.int32, (rows, LANES), 0), tq)
    s_own = jnp.where(col <= trow, _dot(qs.astype(BF16), kn, _NT) + bo_ref[...], NEG)
    m = jnp.maximum(jnp.max(logit, axis=-1, keepdims=True), jnp.max(s_own, axis=-1, keepdims=True))
    pr = jnp.exp(logit - m)
    po = jnp.exp(s_own - m)
    inv = 1.0 / (jnp.sum(pr, axis=-1, keepdims=True) + jnp.sum(po, axis=-1, keepdims=True))
    p_ref[0] = (pr * inv).astype(BF16)
    oo_ref[0] = _dot((po * inv).astype(BF16), vn)


def _dec_select(scores, kmp, q, kn, vn, bias_past, bias_own, expand, *, n_past):
    b, tq, _ = q.shape
    rows = N_HEADS * tq
    plen = scores.shape[2]
    full = lambda a: pl.BlockSpec(a.shape, lambda i: (0,) * a.ndim)
    per = lambda a: pl.BlockSpec((1,) + a.shape[1:], lambda i: (i,) + (0,) * (a.ndim - 1))
    return pl.pallas_call(
        functools.partial(_dec_select_body, tq=tq, n_past=n_past),
        grid=(b,),
        in_specs=[per(scores), per(kmp), per(q), per(kn), per(vn), full(bias_past), full(bias_own),
                  full(expand)],
        out_specs=(pl.BlockSpec((1, rows, plen), lambda i: (i, 0, 0)),
                   pl.BlockSpec((1, rows, WIDTH), lambda i: (i, 0, 0))),
        out_shape=(jax.ShapeDtypeStruct((b, rows, plen), BF16),
                   jax.ShapeDtypeStruct((b, rows, WIDTH), F32)),
        compiler_params=pltpu.CompilerParams(dimension_semantics=("arbitrary",),
                                             vmem_limit_bytes=VMEM_LIMIT),
        name="dec_select",
    )(scores, kmp, q, kn, vn, bias_past, bias_own, expand)


def _dec_pv_body(pt_ref, p_ref, oo_ref, *refs, pps, tq):
    v_refs = refs[:pps]
    o_ref, acc = refs[pps], refs[pps + 1]
    j = pl.program_id(1)

    @pl.when(j == 0)
    def _():
        acc[...] = oo_ref[0]

    tot = acc[...]
    for i in range(pps):
        tot = tot + _dot(p_ref[0, :, i * LANES:(i + 1) * LANES], v_refs[i][0].astype(BF16), _NT)
    acc[...] = tot

    @pl.when(j == pl.num_programs(1) - 1)
    def _():
        masks = _head_masks(WIDTH)
        a = acc[...]
        out = jnp.where(masks[0], a[0:tq], 0.0)
        for h in range(1, N_HEADS):
            out = out + jnp.where(masks[h], a[h * tq:(h + 1) * tq], 0.0)
        o_ref[0] = out


def _dec_pv(pt_flat, probs, o_own, cache, *, npages, pps, tq):
    b, rows, _ = probs.shape
    grid_spec = pltpu.PrefetchScalarGridSpec(
        num_scalar_prefetch=1,
        grid=(b, npages // pps),
        in_specs=[pl.BlockSpec((1, rows, pps * LANES), lambda bi, j, pt: (bi, 0, j)),
                  pl.BlockSpec((1, rows, WIDTH), lambda bi, j, pt: (bi, 0, 0))] + _page_specs(pps, npages),
        out_specs=pl.BlockSpec((1, tq, WIDTH), lambda bi, j, pt: (bi, 0, 0)),
        scratch_shapes=[pltpu.VMEM((rows, WIDTH), F32)],
    )
    return pl.pallas_call(
        functools.partial(_dec_pv_body, pps=pps, tq=tq),
        grid_spec=grid_spec,
        out_shape=jax.ShapeDtypeStruct((b, tq, WIDTH), F32),
        compiler_params=pltpu.CompilerParams(dimension_semantics=("arbitrary", "arbitrary"),
                                             vmem_limit_bytes=VMEM_LIMIT),
        name="dec_pv",
    )(pt_flat, probs, o_own, *([cache] * pps))


def _outproj_body(oa_ref, ob_ref, x_ref, w_ref, g_ref, wr_ref, br_ref, cnt0_ref,
                  x1_ref, xn_ref, gate_ref, exp_ref, cnt_ref, cnt_scr):
    @pl.when(pl.program_id(0) == 0)
    def _():
        cnt_scr[...] = cnt0_ref[...]

    mixed = _dot(oa_ref[...].astype(BF16), w_ref[0:WIDTH, :]) + _dot(ob_ref[...].astype(BF16),
                                                                      w_ref[WIDTH:2 * WIDTH, :])
    x1 = x_ref[...] + mixed
    x1_ref[...] = x1
    ms = jnp.mean(x1 * x1, axis=-1, keepdims=True)
    xn = x1 * lax.rsqrt(ms + RMS_EPS) * g_ref[...]
    xn_ref[...] = xn.astype(BF16)
    logits = _mm3(xn, wr_ref[...]) + br_ref[...]
    col = lax.broadcasted_iota(jnp.int32, logits.shape, 1)
    g = jnp.where(col < N_EXPERTS, logits, -jnp.inf)
    vals, idxs = [], []
    for _ in range(TOP_K):
        mx = jnp.max(g, axis=-1, keepdims=True)
        idx = jnp.min(jnp.where(g == mx, col, jnp.int32(1 << 30)), axis=-1, keepdims=True)
        vals.append(mx)
        idxs.append(idx)
        g = jnp.where(col == idx, -jnp.inf, g)
    es = [jnp.exp(vv - vals[0]) for vv in vals]
    den = es[0]
    for e in es[1:]:
        den = den + e
    tm = logits.shape[0]
    onehot = jnp.zeros(logits.shape, F32)
    for kk in range(TOP_K):
        onehot = onehot + (col == idxs[kk]).astype(F32)
    r_i = lax.broadcasted_iota(jnp.int32, (tm, tm), 0)
    c_i = lax.broadcasted_iota(jnp.int32, (tm, tm), 1)
    prefix = _dot((r_i > c_i).astype(BF16), onehot.astype(BF16)) + cnt_scr[...]
    gates = jnp.zeros(logits.shape, F32)
    experts = jnp.zeros(logits.shape, jnp.int32)
    for kk in range(TOP_K):
        gates = jnp.where(col == kk, es[kk] / den, gates)
        experts = jnp.where(col == kk, idxs[kk], experts)
        rank = jnp.sum(jnp.where(col == idxs[kk], prefix, 0.0), axis=-1, keepdims=True)
        experts = jnp.where(col == TOP_K + kk, rank.astype(jnp.int32), experts)
    gate_ref[...] = gates
    exp_ref[...] = experts
    cnt_scr[...] = cnt_scr[...] + jnp.sum(onehot, axis=0, keepdims=True)
    cnt_ref[...] = cnt_scr[...]


def _outproj(oa, ob, x2d, w_out, g2, wr, br, cnt0):
    n = x2d.shape[0]
    tm = ROW_TILE
    row = lambda w: pl.BlockSpec((tm, w), lambda i: (i, 0))
    full = lambda a: pl.BlockSpec(a.shape, lambda i: (0,) * a.ndim)
    return pl.pallas_call(
        _outproj_body,
        grid=(n // tm,),
        in_specs=[row(WIDTH), row(WIDTH), row(D_MODEL), full(w_out), full(g2), full(wr), full(br), full(cnt0)],
        out_specs=(row(D_MODEL), row(D_MODEL), row(LANES), row(LANES), full(cnt0)),
        out_shape=(jax.ShapeDtypeStruct((n, D_MODEL), F32), jax.ShapeDtypeStruct((n, D_MODEL), BF16),
                   jax.ShapeDtypeStruct((n, LANES), F32), jax.ShapeDtypeStruct((n, LANES), jnp.int32),
                   jax.ShapeDtypeStruct(cnt0.shape, F32)),
        scratch_shapes=[pltpu.VMEM(cnt0.shape, F32)],
        compiler_params=pltpu.CompilerParams(dimension_semantics=("arbitrary",),
                                             vmem_limit_bytes=VMEM_LIMIT),
        name="outproj",
    )(oa, ob, x2d, w_out, g2, wr, br, cnt0)


def _moe_body(be_ref, nv_ref, x_ref, wg_ref, bg_ref, wu_ref, bu_ref, wd_ref, bdn_ref, after_ref, y_ref,
              wg_s, wu_s, wd_s):
    i = pl.program_id(0)
    new_expert = jnp.logical_or(i == 0, be_ref[i] != be_ref[jnp.maximum(i - 1, 0)])

    @pl.when(new_expert)
    def _():
        wg_s[...] = wg_ref[0].astype(BF16)
        wu_s[...] = wu_ref[0].astype(BF16)
        wd_s[...] = wd_ref[0].astype(BF16)

    @pl.when(i < nv_ref[0])
    def _():
        x = x_ref[...]
        gate = jnp.minimum(_dot(x, wg_s[...]) + bg_ref[0], SWIGLU_LIMIT)
        up = jnp.clip(_dot(x, wu_s[...]) + bu_ref[0], -SWIGLU_LIMIT, SWIGLU_LIMIT)
        glu = gate * _sigmoid(SWIGLU_ALPHA * gate)
        hmid = ((up + 1.0) * glu).astype(BF16)
        y_ref[...] = _dot(hmid, wd_s[...]) + bdn_ref[0]

    @pl.when(i >= nv_ref[0])
    def _():
        y_ref[...] = jnp.zeros_like(y_ref)


def _moe(block_e, nvalid, xb, wg, bg, wu, bu, wd, bdn, after, *, rows):
    n_rows = xb.shape[0]
    nb = n_rows // rows
    d_ff = wg.shape[2]
    wspec = lambda s: pl.BlockSpec((1,) + s, lambda i, be, nv: (be[i], 0, 0))
    grid_spec = pltpu.PrefetchScalarGridSpec(
        num_scalar_prefetch=2,
        grid=(nb,),
        in_specs=[pl.BlockSpec((rows, D_MODEL), lambda i, be, nv: (i, 0)),
                  wspec((D_MODEL, d_ff)), wspec((1, d_ff)), wspec((D_MODEL, d_ff)), wspec((1, d_ff)),
                  wspec((d_ff, D_MODEL)), wspec((1, D_MODEL)),
                  pl.BlockSpec(after.shape, lambda i, be, nv: (0, 0))],
        out_specs=pl.BlockSpec((rows, D_MODEL), lambda i, be, nv: (i, 0)),
        scratch_shapes=[pltpu.VMEM((D_MODEL, d_ff), BF16), pltpu.VMEM((D_MODEL, d_ff), BF16),
                        pltpu.VMEM((d_ff, D_MODEL), BF16)],
    )
    return pl.pallas_call(
        _moe_body,
        grid_spec=grid_spec,
        out_shape=jax.ShapeDtypeStruct((n_rows, D_MODEL), F32),
        compiler_params=pltpu.CompilerParams(dimension_semantics=("arbitrary",),
                                             vmem_limit_bytes=VMEM_LIMIT),
        name="moe_ffn",
    )(block_e, nvalid, xb, wg, bg, wu, bu, wd, bdn, after)


def _blockdiag_state(s):
    b = s.shape[0]
    ng = N_HEADS // GROUP
    s5 = s.reshape(b, ng, GROUP, HEAD_DIM, HEAD_DIM)
    eye = jnp.eye(GROUP, dtype=s.dtype)
    return jnp.einsum("bghde,hk->bghdke", s5, eye).reshape(b, ng, GW, GW)


def _unblock_state(sbd):
    b = sbd.shape[0]
    ng = N_HEADS // GROUP
    s6 = sbd.reshape(b, ng, GROUP, HEAD_DIM, GROUP, HEAD_DIM)
    d = jnp.diagonal(s6, axis1=2, axis2=4)
    return jnp.moveaxis(d, -1, 2).reshape(b, N_HEADS, HEAD_DIM, HEAD_DIM)


def _mixer(x, cache_k, cache_v, page_table, s0, cbuf, rel_bias, lw):
    (g1, w_cat, bd, qg, kg, cw, av, dtb, eb, eg, og) = lw
    b, t, _ = x.shape
    x2d = x.reshape(b * t, D_MODEL)
    prompt = cache_k is None
    outs = _inproj(x2d, g1, w_cat, bd, qg, kg, seq_len=t, attn_layout=prompt)
    qkva, z, bg, qb, kb, vb = outs[:6]
    r3 = lambda a: a.reshape(b, t, a.shape[-1])
    qkva3 = r3(qkva)
    c = min(DELTA_CHUNK, t)
    tt = min(GDN_TILE, t)
    o_a, s_fin = _gdn(qkva3, r3(z), r3(bg), cbuf, _blockdiag_state(s0), cw, av, dtb, eb, eg, bd, og,
                      tt=tt, c=c)
    new_conv = jnp.concatenate([cbuf, qkva3], axis=1)[:, t:] if t < CONV_WIDTH - 1 else qkva3[:, t - (CONV_WIDTH - 1):]
    qb3 = r3(qb)
    rb = rel_bias.astype(F32)

    if prompt:
        kbf, vt, km = outs[6:]
        heads_last = lambda a: jnp.transpose(a.reshape(b, N_HEADS, HEAD_DIM, t), (0, 3, 1, 2))
        k4, v4 = heads_last(kb), heads_last(vb)
        nb = t // MOBA_BLOCK
        kmp = jnp.pad(km.reshape(b, nb, WIDTH), ((0, 0), (0, LANES - nb), (0, 0)))
        ii = np.arange(MOBA_BLOCK)
        d_own = ii[None, :] - ii[:, None]
        far_b = _bucket_np(np.arange(MOBA_BLOCK + 1, max(t, MOBA_BLOCK + 2)))
        assert (far_b == far_b[0]).all()
        bias_tab = _bias_table(rel_bias, np.concatenate([d_own, d_own + MOBA_BLOCK], axis=0))
        bias_tab = bias_tab.reshape(N_HEADS, 2, MOBA_BLOCK, MOBA_BLOCK)
        far = rb[int(far_b[0])]
        o_b = _moba_prompt(qb3, r3(kbf), vt, kmp, bias_tab, far).reshape(b * t, WIDTH)
    else:
        npages = page_table.shape[1]
        page = cache_k.shape[1]
        past = npages * page
        assert page == LANES and past % MOBA_BLOCK == 0 and t <= LANES
        n_past = past // MOBA_BLOCK
        pps = next(n for n in (32, 16, 8, 4, 2) if npages % n == 0)
        rows = N_HEADS * t
        pt_flat = page_table.reshape(-1).astype(jnp.int32)
        slab = lambda cch: jnp.transpose(cch, (0, 2, 3, 1)).reshape(cch.shape[0], WIDTH, page)
        kb3, vb3 = r3(kb), r3(vb)
        k4, v4 = (a.reshape(b, t, N_HEADS, HEAD_DIM) for a in (kb3, vb3))
        scores, kmt = _dec_scores(pt_flat, qb3, slab(cache_k), npages=npages, pps=pps)
        tpos = np.arange(t)
        d_past = past + tpos[:, None] - np.arange(past)[None, :]
        bias_past = _bias_table(rel_bias, d_past).reshape(rows, past)
        d_own = np.zeros((t, LANES), np.int64)
        d_own[:, :t] = tpos[:, None] - tpos[None, :]
        bias_own = _bias_table(rel_bias, d_own).reshape(rows, LANES)
        ex = np.zeros((LANES, past), np.float32)
        ex[np.arange(past) // MOBA_BLOCK, np.arange(past)] = 1.0
        probs, o_own = _dec_select(scores, kmt, qb3, kb3, vb3, bias_past, bias_own, jnp.asarray(ex, BF16),
                                   n_past=n_past)
        o_b = _dec_pv(pt_flat, probs, o_own, slab(cache_v), npages=npages, pps=pps, tq=t).reshape(b * t, WIDTH)
    marker = None if prompt else scores[0, :SUBLANES, :LANES]
    return o_a.reshape(b * t, WIDTH), o_b, k4, v4, _unblock_state(s_fin), new_conv, marker


def _moe_ffn(xn, gates, er, counts, x1, rows, after, wg, bg, wu, bu, wd, bdn):
    n = xn.shape[0]
    nk = n * TOP_K
    flat_e = er[:, :TOP_K].reshape(nk)
    rank = er[:, TOP_K:2 * TOP_K].reshape(nk)
    counts = counts[0, :N_EXPERTS].astype(jnp.int32)
    padded = ((counts + rows - 1) // rows) * rows
    pad_end = jnp.cumsum(padded)
    pad_start = pad_end - padded
    dest = pad_start[flat_e] + rank
    n_blocks = -(-nk // rows) + N_EXPERTS
    n_rows = n_blocks * rows
    flat_tok = jnp.arange(nk, dtype=jnp.int32) // TOP_K
    slot_tok = jnp.zeros((n_rows,), jnp.int32).at[dest].set(flat_tok, unique_indices=True)
    nvalid = (pad_end[-1] // rows).astype(jnp.int32).reshape(1)
    blk_start = jnp.arange(n_blocks, dtype=jnp.int32) * rows
    block_e = jnp.sum((blk_start[:, None] >= pad_end[None, :]).astype(jnp.int32), axis=1)
    block_e = jnp.minimum(block_e, N_EXPERTS - 1)
    last_e = block_e[jnp.maximum(nvalid[0] - 1, 0)]
    block_e = jnp.where(jnp.arange(n_blocks) < nvalid[0], block_e, last_e)
    yb = _moe(block_e, nvalid, xn[slot_tok], wg, bg, wu, bu, wd, bdn, after, rows=rows)
    dest2 = dest.reshape(n, TOP_K)
    y = x1
    for kk in range(TOP_K):
        y = y + gates[:, kk:kk + 1] * yb[dest2[:, kk]]
    return y


def kernel(x_prompt, x_sample, cache_k, cache_v, state_delta, state_conv, page_table, rel_bias,
           norm1_g, w_in, conv_w, A_log, dt_bias, o_norm_g, q_norm_g, k_norm_g, w_out, norm2_g,
           w_router, b_router, w_gate, b_gate, w_up, b_up, w_down, b_down):
    depth = norm1_g.shape[0]
    bp, tp, _ = x_prompt.shape
    bs, ts, _ = x_sample.shape
    yp, ys = x_prompt, x_sample
    outs = [[] for _ in range(8)]
    seg = np.arange(WIDTH) // HEAD_DIM
    segw = np.arange(GW) // HEAD_DIM
    bd = jnp.asarray((segw[:, None] == segw[None, :]).astype(np.float32) / HEAD_DIM, BF16)
    eb_np = np.zeros((LANES, WIDTH), np.float32)
    eb_np[seg, np.arange(WIDTH)] = 1.0
    eg_np = np.zeros((LANES, WIDTH), np.float32)
    eg_np[N_HEADS + seg, np.arange(WIDTH)] = 1.0
    eb, eg = jnp.asarray(eb_np, BF16), jnp.asarray(eg_np, BF16)
    c1, c3 = QKV + WIDTH, QKV + WIDTH + 2 * N_HEADS
    for l in range(depth):
        wl = w_in[l]
        w_cat = jnp.concatenate([wl[:, :c1], wl[:, c3:], wl[:, c1:c3],
                                 jnp.zeros((D_MODEL, LANES - 2 * N_HEADS), wl.dtype)], axis=1).astype(BF16)
        tile8 = lambda g: jnp.tile(g.astype(F32), N_HEADS).reshape(1, WIDTH)
        av = jnp.zeros((1, LANES), F32).at[0, N_HEADS:2 * N_HEADS].set(-jnp.exp(A_log[l].astype(F32)))
        dtb = jnp.zeros((1, LANES), F32).at[0, N_HEADS:2 * N_HEADS].set(dt_bias[l].astype(F32))
        lw = (norm1_g[l].astype(F32).reshape(1, D_MODEL), w_cat, bd, tile8(q_norm_g[l]), tile8(k_norm_g[l]),
              conv_w[l].astype(F32), av, dtb, eb, eg, tile8(o_norm_g[l]))
        w_o = w_out[l].astype(BF16)
        g2 = norm2_g[l].astype(F32).reshape(1, D_MODEL)
        wr = jnp.pad(w_router[l].astype(F32), ((0, 0), (0, LANES - N_EXPERTS)))
        br = jnp.pad(b_router[l].astype(F32), (0, LANES - N_EXPERTS)).reshape(1, LANES)
        cnt0 = jnp.zeros((1, LANES), F32)
        ffn_w = (w_gate[l].astype(F32), b_gate[l].astype(F32)[:, None, :],
                 w_up[l].astype(F32), b_up[l].astype(F32)[:, None, :],
                 w_down[l].astype(F32), b_down[l].astype(F32)[:, None, :])

        def ffn(oa, ob, x, after):
            n = x.shape[0] * x.shape[1]
            x1, xn, gt, er, cnt = _outproj(oa, ob, x.reshape(n, D_MODEL), w_o, g2, wr, br, cnt0)
            rows = MOE_ROWS if n * TOP_K >= 2 * N_EXPERTS * MOE_ROWS else MOE_ROWS_SMALL
            return _moe_ffn(xn, gt, er, cnt, x1, rows, after, *ffn_w).reshape(x.shape)

        s0_p = jnp.zeros((bp, N_HEADS, HEAD_DIM, HEAD_DIM), F32)
        c0_p = jnp.zeros((bp, CONV_WIDTH - 1, QKV), F32)
        oa_p, ob_p, k_p, v_p, s_p, c_p, _ = _mixer(yp, None, None, None, s0_p, c0_p, rel_bias, lw)
        oa_s, ob_s, k_s, v_s, s_s, c_s, dec_marker = _mixer(ys, cache_k[l], cache_v[l], page_table,
                                                            state_delta[l], state_conv[l], rel_bias, lw)
        yp = ffn(oa_p, ob_p, yp, dec_marker)
        ys = ffn(oa_s, ob_s, ys, jnp.zeros((SUBLANES, LANES), F32))
        for lst, val in zip(outs, (k_p, v_p, s_p, c_p, k_s, v_s, s_s, c_s)):
            lst.append(val)
    stacked = [jnp.stack(o) for o in outs]
    return (yp, ys, *stacked)
```

```python
import functools
import math

import numpy as np
import jax
import jax.numpy as jnp
from jax import lax
from jax.experimental import pallas as pl
from jax.experimental.pallas import tpu as pltpu

F32 = jnp.float32
BF16 = jnp.bfloat16

D_MODEL = 1024
N_HEADS = 8
HEAD_DIM = 64
WIDTH = N_HEADS * HEAD_DIM
QKV = 3 * WIDTH
CONV_WIDTH = 4
DELTA_CHUNK = 64
MOBA_BLOCK = 256
MOBA_TOP = 3
ATTN_SCALE = HEAD_DIM ** -0.5
NUM_BUCKETS = 32
MAX_DISTANCE = 128
N_EXPERTS = 32
TOP_K = 4
SWIGLU_LIMIT = 7.0
SWIGLU_ALPHA = 1.702
RMS_EPS = 1e-6

LANES = 128
SUBLANES = 8
ROW_TILE = 256
GDN_TILE = 256
MOE_ROWS = 512
MOE_ROWS_SMALL = 128
NEG = -1e30
VMEM_LIMIT = 48 * 1024 * 1024

GROUP = 4
GW = GROUP * HEAD_DIM
NPAIR = WIDTH // LANES
FAR_BLOCKS = 4
KEY_SUB = 128
SCORE_LOOKAHEAD = 5

_NN = (((1,), (0,)), ((), ()))
_NT = (((1,), (1,)), ((), ()))
_TN = (((0,), (0,)), ((), ()))


def _dot(a, b, dims=_NN):
    return lax.dot_general(a, b, dims, preferred_element_type=F32)


def _split2(a):
    hi = a.astype(BF16)
    lo = (a - hi.astype(F32)).astype(BF16)
    return hi, lo


def _split3(a):
    hi = a.astype(BF16)
    r = a - hi.astype(F32)
    mid = r.astype(BF16)
    lo = (r - mid.astype(F32)).astype(BF16)
    return hi, mid, lo


def _mm1(a, b, dims=_NN):
    return _dot(a.astype(BF16), b.astype(BF16), dims)


def _mm3(a, b, dims=_NN):
    ah, al = _split2(a)
    bh, bl = _split2(b)
    return _dot(ah, bh, dims) + (_dot(ah, bl, dims) + _dot(al, bh, dims))


def _mm_exact_rhs(a, b_bf16, dims=_NN):
    hi, mid, lo = _split3(a)
    return _dot(hi, b_bf16, dims) + (_dot(mid, b_bf16, dims) + _dot(lo, b_bf16, dims))


def _mm_exact_lhs(a_bf16, b, dims=_NN):
    hi, mid, lo = _split3(b)
    return _dot(a_bf16, hi, dims) + (_dot(a_bf16, mid, dims) + _dot(a_bf16, lo, dims))


def _seg_mean(x, bd):
    outs = []
    for g in range(x.shape[1] // GW):
        hi, lo = _split2(x[:, g * GW:(g + 1) * GW])
        outs.append(_dot(hi, bd) + _dot(lo, bd))
    return outs[0] if len(outs) == 1 else jnp.concatenate(outs, axis=1)


def _sigmoid(x):
    return 1.0 / (1.0 + jnp.exp(-x))


def _softplus(x):
    return jnp.maximum(x, 0.0) + jnp.log(1.0 + jnp.exp(-jnp.abs(x)))


def _inproj_body(x_ref, g_ref, w_ref, bd_ref, qg_ref, kg_ref,
                 qkva_ref, z_ref, bg_ref, qb_ref, kb_ref, vb_ref, *attn_refs):
    x = x_ref[...]
    ms = jnp.mean(x * x, axis=-1, keepdims=True)
    h = (x * lax.rsqrt(ms + RMS_EPS) * g_ref[...]).astype(BF16)

    def proj(lo, hi):
        return _dot(h, w_ref[:, lo:hi])

    bd = bd_ref[...]
    qkva_ref[...] = proj(0, QKV)
    z_ref[...] = proj(QKV, QKV + WIDTH)
    c = QKV + WIDTH
    q = proj(c, c + WIDTH)
    qb_ref[...] = q * lax.rsqrt(_seg_mean(q * q, bd) + RMS_EPS) * qg_ref[...]
    k = proj(c + WIDTH, c + 2 * WIDTH)
    kn = k * lax.rsqrt(_seg_mean(k * k, bd) + RMS_EPS) * kg_ref[...]
    v = proj(c + 2 * WIDTH, c + 3 * WIDTH)
    bg_ref[...] = proj(c + 3 * WIDTH, c + 3 * WIDTH + LANES)
    if attn_refs:
        kbf_ref, vt_ref, km_ref = attn_refs
        kb_ref[0] = kn.T
        vt = v.T
        vb_ref[0] = vt
        kbf_ref[...] = kn.astype(BF16)
        km_ref[0] = jnp.mean(kn, axis=0, keepdims=True)
        vtb = vt.astype(BF16)
        for pp in range(NPAIR):
            vt_ref[0, pp, 0] = vtb[pp * LANES:(pp + 1) * LANES, :]
    else:
        kb_ref[...] = kn
        vb_ref[...] = v


def _inproj(x2d, g1, w_cat, bd, qg, kg, *, seq_len, attn_layout):
    n = x2d.shape[0]
    tm = ROW_TILE
    nt = n // tm
    row = lambda w: pl.BlockSpec((tm, w), lambda i: (i, 0))
    full = lambda a: pl.BlockSpec(a.shape, lambda i: (0,) * a.ndim)
    out_shape = [
        jax.ShapeDtypeStruct((n, QKV), F32), jax.ShapeDtypeStruct((n, WIDTH), F32),
        jax.ShapeDtypeStruct((n, LANES), F32), jax.ShapeDtypeStruct((n, WIDTH), F32),
    ]
    out_specs = [row(QKV), row(WIDTH), row(LANES), row(WIDTH)]
    if not attn_layout:
        out_shape += [jax.ShapeDtypeStruct((n, WIDTH), F32)] * 2
        out_specs += [row(WIDTH)] * 2
    else:
        tpb = seq_len // tm
        kv_t = jax.ShapeDtypeStruct((n // seq_len, WIDTH, seq_len), F32)
        kv_spec = pl.BlockSpec((1, WIDTH, tm), lambda i: (i // tpb, 0, i % tpb))
        out_shape += [kv_t, kv_t]
        out_specs += [kv_spec, kv_spec]
        out_shape += [jax.ShapeDtypeStruct((n, WIDTH), BF16),
                      jax.ShapeDtypeStruct((n // seq_len, NPAIR, tpb, LANES, tm), BF16),
                      jax.ShapeDtypeStruct((nt, 1, WIDTH), F32)]
        out_specs += [row(WIDTH),
                      pl.BlockSpec((1, NPAIR, 1, LANES, tm), lambda i: (i // tpb, 0, i % tpb, 0, 0)),
                      pl.BlockSpec((1, 1, WIDTH), lambda i: (i, 0, 0))]
    return pl.pallas_call(
        _inproj_body,
        grid=(nt,),
        in_specs=[row(D_MODEL), full(g1), full(w_cat), full(bd), full(qg), full(kg)],
        out_specs=tuple(out_specs),
        out_shape=tuple(out_shape),
        compiler_params=pltpu.CompilerParams(dimension_semantics=("parallel",),
                                             vmem_limit_bytes=VMEM_LIMIT),
        name="inproj",
    )(x2d, g1, w_cat, bd, qg, kg)


def _stack(x, masks):
    return jnp.concatenate([jnp.where(m, x, 0.0) for m in masks], axis=0)


def _unstack(w, c, n):
    acc = w[0:c]
    for h in range(1, n):
        acc = acc + w[h * c:(h + 1) * c]
    return acc


def _gdn_body(qkva_ref, z_ref, bg_ref, cbuf_ref, s0_ref, cw_ref, av_ref, dtb_ref, eb_ref, eg_ref,
              bd_ref, og_ref, o_ref, sfin_ref, s_scr, carry_scr, uext_scr, *, tt, c):
    t = pl.program_id(1)
    nt = pl.num_programs(1)
    gc = GROUP * c

    @pl.when(t == 0)
    def _():
        s_scr[...] = s0_ref[0]
        carry_scr[...] = jnp.zeros_like(carry_scr)
        carry_scr[SUBLANES - (CONV_WIDTH - 1):SUBLANES, :] = cbuf_ref[0]

    uext_scr[0:SUBLANES, :] = carry_scr[...]
    uext_scr[SUBLANES:SUBLANES + tt, :] = qkva_ref[0]
    carry_scr[...] = uext_scr[tt:tt + SUBLANES, :]

    cw = cw_ref[...]
    off = SUBLANES - (CONV_WIDTH - 1)
    conv = cw[0:1] * uext_scr[off:off + tt, :]
    for i in range(1, CONV_WIDTH):
        conv = conv + cw[i:i + 1] * uext_scr[off + i:off + i + tt, :]
    act = conv * _sigmoid(conv)

    bd = bd_ref[...]
    q = act[:, 0:WIDTH]
    k = act[:, WIDTH:2 * WIDTH]
    v = act[:, 2 * WIDTH:3 * WIDTH]
    q = q * lax.rsqrt(_seg_mean(q * q, bd) * HEAD_DIM + RMS_EPS) * (HEAD_DIM ** -0.5)
    k = k * lax.rsqrt(_seg_mean(k * k, bd) * HEAD_DIM + RMS_EPS)

    bg = bg_ref[0]
    beta = _mm_exact_rhs(_sigmoid(bg), eb_ref[...])
    gx = _mm_exact_rhs(av_ref[...] * _softplus(bg + dtb_ref[...]), eg_ref[...])

    r4 = lax.broadcasted_iota(jnp.int32, (gc, gc), 0)
    c4 = lax.broadcasted_iota(jnp.int32, (gc, gc), 1)
    same = (r4 // c) == (c4 // c)
    tri = jnp.logical_and(same, r4 >= c4)
    strict = jnp.logical_and(same, r4 > c4)
    eye = (r4 == c4).astype(F32)
    rc = lax.broadcasted_iota(jnp.int32, (c, c), 0)
    cc = lax.broadcasted_iota(jnp.int32, (c, c), 1)
    tril_c = (rc >= cc).astype(BF16)
    lane_g = lax.broadcasted_iota(jnp.int32, (1, GW), 1) // HEAD_DIM
    hmasks = [lane_g == h for h in range(GROUP)]
    rs = lax.broadcasted_iota(jnp.int32, (GW, GW), 0) // HEAD_DIM
    cs = lax.broadcasted_iota(jnp.int32, (GW, GW), 1) // HEAD_DIM
    bdm = (rs == cs).astype(F32)
    nsq = int(round(math.log2(c))) - 1

    nch, ngr = tt // c, N_HEADS // GROUP
    inst = []
    for ch in range(nch):
        r0 = ch * c
        gcum = _mm_exact_lhs(tril_c, gx[r0:r0 + c])
        eg = jnp.exp(gcum)
        glast = gcum[c - 1:c, :]
        kscale = jnp.exp(glast - gcum)
        gtot = jnp.exp(glast)
        for gr in range(ngr):
            sl = slice(gr * GW, (gr + 1) * GW)
            kq = k[r0:r0 + c, sl]
            qq = q[r0:r0 + c, sl]
            bb = beta[r0:r0 + c, sl]
            gq = gcum[:, sl]
            kb = kq * bb
            xk = _stack(kq, hmasks).astype(BF16)
            kk = _dot(_stack(kb, hmasks).astype(BF16), xk, _NT)
            qk = _dot(_stack(qq, hmasks).astype(BF16), xk, _NT)
            gcol = jnp.concatenate(
                [jnp.broadcast_to(gq[:, h * HEAD_DIM:h * HEAD_DIM + 1], (c, gc)) for h in range(GROUP)], axis=0)
            if gc % LANES == 0:
                grow = gcol.T
            else:
                grow = _mm_exact_lhs(jnp.full((gc, GW), 1.0 / HEAD_DIM, BF16), _stack(gq, hmasks), _NT)
            dm = jnp.where(tri, jnp.exp(jnp.where(tri, gcol - grow, 0.0)), 0.0)
            lmat = jnp.where(strict, kk * dm, 0.0)
            rhs = jnp.concatenate([_stack(v[r0:r0 + c, sl] * bb, hmasks), _stack(kb * eg[:, sl], hmasks)], axis=1)
            inst.append(dict(gr=gr, sl=sl, amat=qk * dm, pinv=eye - lmat, msq=lmat, rhs=rhs,
                             qe=qq * eg[:, sl], kd=kq * kscale[:, sl], gtot=gtot[:, sl]))
    for _ in range(nsq):
        for it in inst:
            it["msq"] = _mm1(it["msq"], it["msq"])
        for it in inst:
            it["pinv"] = it["pinv"] + _mm1(it["pinv"], it["msq"])
    for it in inst:
        w = _mm1(it["pinv"], it["rhs"])
        it["value"] = _unstack(w[:, 0:GW], c, GROUP)
        it["kcum"] = _unstack(w[:, GW:2 * GW], c, GROUP)
    o_rows = []
    for ch in range(nch):
        o_groups = []
        for it in inst[ch * ngr:(ch + 1) * ngr]:
            s = s_scr[it["gr"]]
            u = it["value"] - _mm3(it["kcum"], s)
            o_groups.append(_mm1(it["qe"], s) + _unstack(_mm1(it["amat"], _stack(u, hmasks)), c, GROUP))
            s_scr[it["gr"]] = s * it["gtot"] + bdm * _mm3(it["kd"], u, _TN)
        o_rows.append(jnp.concatenate(o_groups, axis=1))
    o = o_rows[0] if len(o_rows) == 1 else jnp.concatenate(o_rows, axis=0)
    on = o * lax.rsqrt(_seg_mean(o * o, bd) + RMS_EPS) * og_ref[...]
    z = z_ref[0]
    o_ref[0] = on * (z * _sigmoid(z))

    @pl.when(t == nt - 1)
    def _():
        sfin_ref[0] = s_scr[...]


def _gdn(qkva, z, bg, cbuf, s0bd, cw, av, dtb, eb, eg, bd, og, *, tt, c):
    b, t, _ = qkva.shape
    ng = N_HEADS // GROUP
    full = lambda a: pl.BlockSpec(a.shape, lambda i, j: (0,) * a.ndim)
    body = functools.partial(_gdn_body, tt=tt, c=c)
    return pl.pallas_call(
        body,
        grid=(b, t // tt),
        in_specs=[
            pl.BlockSpec((1, tt, QKV), lambda i, j: (i, j, 0)),
            pl.BlockSpec((1, tt, WIDTH), lambda i, j: (i, j, 0)),
            pl.BlockSpec((1, tt, LANES), lambda i, j: (i, j, 0)),
            pl.BlockSpec((1, CONV_WIDTH - 1, QKV), lambda i, j: (i, 0, 0)),
            pl.BlockSpec((1, ng, GW, GW), lambda i, j: (i, 0, 0, 0)),
            full(cw), full(av), full(dtb), full(eb), full(eg), full(bd), full(og),
        ],
        out_specs=(pl.BlockSpec((1, tt, WIDTH), lambda i, j: (i, j, 0)),
                   pl.BlockSpec((1, ng, GW, GW), lambda i, j: (i, 0, 0, 0))),
        out_shape=(jax.ShapeDtypeStruct((b, t, WIDTH), F32),
                   jax.ShapeDtypeStruct((b, ng, GW, GW), F32)),
        scratch_shapes=[pltpu.VMEM((ng, GW, GW), F32), pltpu.VMEM((SUBLANES, QKV), F32),
                        pltpu.VMEM((tt + SUBLANES, QKV), F32)],
        compiler_params=pltpu.CompilerParams(dimension_semantics=("arbitrary", "arbitrary"),
                                             vmem_limit_bytes=VMEM_LIMIT),
        name="gdn",
    )(qkva, z, bg, cbuf, s0bd, cw, av, dtb, eb, eg, bd, og)


def _bucket_np(dist):
    n = np.maximum(dist, 0)
    max_exact = NUM_BUCKETS // 2
    n_f = np.maximum(n, max_exact).astype(np.float32)
    large = max_exact + (np.log(n_f / np.float32(max_exact)) / np.float32(math.log(MAX_DISTANCE / max_exact))
                         * np.float32(NUM_BUCKETS - max_exact)).astype(np.int32)
    large = np.minimum(large, NUM_BUCKETS - 1)
    return np.where(n < max_exact, n, large).astype(np.int32)


def _bucket_steps(max_dist):
    b = _bucket_np(np.arange(max_dist + 1))
    assert (np.diff(b) >= 0).all()
    firsts = np.flatnonzero(np.diff(b, prepend=-1))
    return [(int(d), int(b[d])) for d in firsts]


def _bias_body(rb_ref, d_ref, o_ref, *, steps):
    h = pl.program_id(0)
    d = d_ref[...]
    acc = jnp.full(d.shape, rb_ref[steps[0][1] * N_HEADS + h], F32)
    for first, bucket in steps[1:]:
        acc = jnp.where(d >= first, rb_ref[bucket * N_HEADS + h], acc)
    o_ref[0] = acc


def _bias_table(rel_bias, dist_np):
    r, c = dist_np.shape
    steps = _bucket_steps(int(max(dist_np.max(), 0)))
    grid_spec = pltpu.PrefetchScalarGridSpec(
        num_scalar_prefetch=1,
        grid=(N_HEADS,),
        in_specs=[pl.BlockSpec((r, c), lambda h, rb: (0, 0))],
        out_specs=pl.BlockSpec((1, r, c), lambda h, rb: (h, 0, 0)),
    )
    return pl.pallas_call(
        functools.partial(_bias_body, steps=steps),
        grid_spec=grid_spec,
        out_shape=jax.ShapeDtypeStruct((N_HEADS, r, c), F32),
        compiler_params=pltpu.CompilerParams(dimension_semantics=("arbitrary",),
                                             vmem_limit_bytes=VMEM_LIMIT),
        name="bias_table",
    )(rel_bias.astype(F32).reshape(-1), jnp.asarray(np.maximum(dist_np, 0), jnp.int32))


def _top_select(gate, idx, n_cand, n_valid_f, axis):
    g = jnp.where(idx < n_cand, gate, -jnp.inf)
    sel = jnp.zeros(gate.shape, F32)
    for r in range(MOBA_TOP):
        mx = jnp.max(g, axis=axis, keepdims=True)
        first = jnp.min(jnp.where(g == mx, idx, jnp.int32(1 << 30)), axis=axis, keepdims=True)
        hit = idx == first
        sel = jnp.maximum(sel, jnp.where(hit, n_valid_f[r], 0.0))
        g = jnp.where(hit, -jnp.inf, g)
    return sel


def _moba_prompt_body(far_ref, q_ref, k_ref, vt_ref, km_ref, bias_ref, o_ref, sel_scr):
    p = pl.program_id(1)
    i = pl.program_id(2)
    blk = MOBA_BLOCK
    qt = (q_ref[0] * ATTN_SCALE).T
    row = lax.broadcasted_iota(jnp.int32, (LANES, blk), 0)
    top = row < HEAD_DIM
    qt_m = [jnp.where(top, qt, 0.0), jnp.where(top, 0.0, qt)]
    qtb = [x.astype(BF16) for x in qt_m]
    km = km_ref[0]
    n_valid = [(i > r).astype(F32) for r in range(MOBA_TOP)]
    for hh in range(2):
        sel_scr[hh] = _top_select(_mm3(km, qt_m[hh]), row, i, n_valid, 0)
    sub = KEY_SUB

    def step(n, carry, nblk):
        ms, ls, accs = [list(x) for x in carry]

        def qk(j, u, hh):
            start = pl.multiple_of((n + j) * blk + u * sub, sub)
            return _dot(k_ref[0, pl.ds(start, sub), :], qtb[hh])

        def absorb(j, u, hh, s):
            shift = far_ref[2 * p + hh]
            valid = sel_scr[hh, pl.ds(n + j, 1), :] > 0.5
            bm = jnp.where(valid, jnp.max(s, axis=0, keepdims=True) + shift, NEG)
            m_new = jnp.maximum(ms[hh], bm)
            alpha = jnp.exp(ms[hh] - m_new)
            off = jnp.where(valid, m_new - shift, -NEG)
            pr = jnp.exp(s - off)
            ls[hh] = alpha * ls[hh] + jnp.sum(pr, axis=0, keepdims=True)
            ms[hh] = m_new
            vh = vt_ref[0, 0, n + j, hh * HEAD_DIM:(hh + 1) * HEAD_DIM, u * sub:(u + 1) * sub]
            accs[hh] = accs[hh] * alpha + _dot(vh, pr.astype(BF16))

        tiles = [(j, u, hh) for j in range(nblk) for u in range(blk // sub) for hh in range(2)]
        pending = []
        for tile in tiles:
            pending.append(tile + (qk(*tile),))
            if len(pending) > SCORE_LOOKAHEAD:
                absorb(*pending.pop(0))
        for item in pending:
            absorb(*item)
        return tuple(ms), tuple(ls), tuple(accs)

    init = ((jnp.full((1, blk), NEG, F32),) * 2, (jnp.zeros((1, blk), F32),) * 2,
            (jnp.zeros((HEAD_DIM, blk), F32),) * 2)
    n_far = jnp.maximum(i - 1, 0)
    n_wide = n_far // FAR_BLOCKS
    carry = lax.fori_loop(0, n_wide, lambda w, cr: step(w * FAR_BLOCKS, cr, FAR_BLOCKS), init)
    done = n_wide * FAR_BLOCKS
    width = FAR_BLOCKS // 2
    while width >= 1:
        take = ((n_far - done) >= width).astype(jnp.int32)
        carry = lax.fori_loop(0, take, lambda _, cr, d=done, w=width: step(d, cr, w), carry)
        done = done + take * width
        width //= 2
    rowb = lax.broadcasted_iota(jnp.int32, (blk, blk), 0)
    colb = lax.broadcasted_iota(jnp.int32, (blk, blk), 1)

    def last_step(n, carry, modes):
        ms, ls, accs = [list(x) for x in carry]
        for hh in range(2):
            ss, valids, bmax = [], [], []
            for j, mode in enumerate(modes):
                s = _dot(k_ref[0, pl.ds(pl.multiple_of((n + j) * blk, blk), blk), :], qtb[hh])
                if mode == "own":
                    s = jnp.where(rowb <= colb, s + bias_ref[hh, 0], NEG)
                    valid = None
                    bm = jnp.max(s, axis=0, keepdims=True)
                else:
                    s = s + bias_ref[hh, 1]
                    valid = sel_scr[hh, pl.ds(n + j, 1), :] > 0.5
                    bm = jnp.where(valid, jnp.max(s, axis=0, keepdims=True), NEG)
                ss.append(s)
                valids.append(valid)
                bmax.append(bm)
            m_new = ms[hh]
            for bm in bmax:
                m_new = jnp.maximum(m_new, bm)
            alpha = jnp.exp(ms[hh] - m_new)
            lsum = alpha * ls[hh]
            acc = accs[hh] * alpha
            for j, s in enumerate(ss):
                off = m_new if valids[j] is None else jnp.where(valids[j], m_new, -NEG)
                pr = jnp.exp(s - off)
                lsum = lsum + jnp.sum(pr, axis=0, keepdims=True)
                acc = acc + _dot(vt_ref[0, 0, n + j, hh * HEAD_DIM:(hh + 1) * HEAD_DIM, :], pr.astype(BF16))
            ls[hh], ms[hh], accs[hh] = lsum, m_new, acc
        return tuple(ms), tuple(ls), tuple(accs)

    has_prev = (i >= 1).astype(jnp.int32)
    carry = lax.fori_loop(0, has_prev, lambda _, cr: last_step(i - 1, cr, ("prev", "own")), carry)
    ms, ls, accs = lax.fori_loop(0, 1 - has_prev, lambda _, cr: last_step(i, cr, ("own",)), carry)
    o_ref[0] = jnp.concatenate([accs[0] / ls[0], accs[1] / ls[1]], axis=0).T


def _moba_prompt(qb, kbf, vt, kmp, bias_tab, far):
    b, t, _ = qb.shape
    nq = t // MOBA_BLOCK
    grid_spec = pltpu.PrefetchScalarGridSpec(
        num_scalar_prefetch=1,
        grid=(b, NPAIR, nq),
        in_specs=[
            pl.BlockSpec((1, MOBA_BLOCK, LANES), lambda bi, p, i, far: (bi, i, p)),
            pl.BlockSpec((1, t, LANES), lambda bi, p, i, far: (bi, 0, p)),
            pl.BlockSpec((1, 1, nq, LANES, MOBA_BLOCK), lambda bi, p, i, far: (bi, p, 0, 0, 0)),
            pl.BlockSpec((1, LANES, LANES), lambda bi, p, i, far: (bi, 0, p)),
            pl.BlockSpec((2, 2, MOBA_BLOCK, MOBA_BLOCK), lambda bi, p, i, far: (p, 0, 0, 0)),
        ],
        out_specs=pl.BlockSpec((1, MOBA_BLOCK, LANES), lambda bi, p, i, far: (bi, i, p)),
        scratch_shapes=[pltpu.VMEM((2, LANES, MOBA_BLOCK), F32)],
    )
    return pl.pallas_call(
        _moba_prompt_body,
        grid_spec=grid_spec,
        out_shape=jax.ShapeDtypeStruct((b, t, WIDTH), F32),
        compiler_params=pltpu.CompilerParams(
            dimension_semantics=("arbitrary", "arbitrary", "arbitrary"),
            vmem_limit_bytes=VMEM_LIMIT),
        name="moba_prompt",
    )(far, qb, kbf, vt, kmp, bias_tab)


def _head_masks(width):
    lane_h = lax.broadcasted_iota(jnp.int32, (1, width), 1) // HEAD_DIM
    return [lane_h == h for h in range(width // HEAD_DIM)]


def _dec_scores_body(pt_ref, q_ref, *refs, pps):
    k_refs = refs[:pps]
    s_ref, kmt_ref = refs[pps], refs[pps + 1]
    masks = _head_masks(WIDTH)
    qs = _stack(q_ref[0] * ATTN_SCALE, masks).astype(BF16)
    ppb = MOBA_BLOCK // LANES
    j = pl.program_id(1)
    for i in range(pps):
        s_ref[0, :, i * LANES:(i + 1) * LANES] = _dot(qs, k_refs[i][0].astype(BF16))
    lane = lax.broadcasted_iota(jnp.int32, (1, LANES), 1)
    @pl.when(j == 0)
    def _():
        kmt_ref[0] = jnp.zeros((WIDTH, LANES), F32)

    kmt = kmt_ref[0]
    for blk in range(pps // ppb):
        tot = k_refs[blk * ppb][0]
        for jj in range(1, ppb):
            tot = tot + k_refs[blk * ppb + jj][0]
        mean = jnp.sum(tot, axis=1, keepdims=True) * (1.0 / MOBA_BLOCK)
        kmt = jnp.where(lane == j * (pps // ppb) + blk, mean, kmt)
    kmt_ref[0] = kmt


def _page_specs(pps, npages):
    def mk(i):
        return pl.BlockSpec((1, WIDTH, LANES), lambda b, j, pt: (pt[b * npages + j * pps + i], 0, 0))
    return [mk(i) for i in range(pps)]


def _dec_scores(pt_flat, q, cache, *, npages, pps):
    b, tq, _ = q.shape
    ppb = MOBA_BLOCK // LANES
    nj = npages // pps
    grid_spec = pltpu.PrefetchScalarGridSpec(
        num_scalar_prefetch=1,
        grid=(b, nj),
        in_specs=[pl.BlockSpec((1, tq, WIDTH), lambda bi, j, pt: (bi, 0, 0))] + _page_specs(pps, npages),
        out_specs=(pl.BlockSpec((1, N_HEADS * tq, pps * LANES), lambda bi, j, pt: (bi, 0, j)),
                   pl.BlockSpec((1, WIDTH, LANES), lambda bi, j, pt: (bi, 0, 0))),
    )
    assert npages // ppb <= LANES
    return pl.pallas_call(
        functools.partial(_dec_scores_body, pps=pps),
        grid_spec=grid_spec,
        out_shape=(jax.ShapeDtypeStruct((b, N_HEADS * tq, npages * LANES), F32),
                   jax.ShapeDtypeStruct((b, WIDTH, LANES), F32)),
        compiler_params=pltpu.CompilerParams(dimension_semantics=("arbitrary", "arbitrary"),
                                             vmem_limit_bytes=VMEM_LIMIT),
        name="dec_scores",
    )(pt_flat, q, *([cache] * pps))


def _dec_select_body(s_ref, km_ref, q_ref, kn_ref, vn_ref, bp_ref, bo_ref, ex_ref, p_ref, oo_ref,
                     *, tq, n_past):
    rows = N_HEADS * tq
    masks = _head_masks(WIDTH)
    qs = _stack(q_ref[0] * ATTN_SCALE, masks)
    col = lax.broadcasted_iota(jnp.int32, (rows, LANES), 1)
    n_valid = [jnp.float32(1.0 if n_past > r else 0.0) for r in range(MOBA_TOP)]
    sel = _top_select(_mm3(qs, km_ref[0]), col, n_past, n_valid, -1)
    selk = _dot(sel.astype(BF16), ex_ref[...])
    logit = jnp.where(selk > 0.5, s_ref[0] + bp_ref[...], NEG)
    pad = jnp.zeros((LANES - tq, WIDTH), F32)
    kn = jnp.concatenate([kn_ref[0], pad], axis=0).astype(BF16)
    vn = jnp.concatenate([vn_ref[0], pad], axis=0).astype(BF16)
    trow = lax.rem(lax.broadcasted_iota(jnp.int32, (rows, LANES), 0), tq)
    s_own = jnp.where(col <= trow, _dot(qs.astype(BF16), kn, _NT) + bo_ref[...], NEG)
    m = jnp.maximum(jnp.max(logit, axis=-1, keepdims=True), jnp.max(s_own, axis=-1, keepdims=True))
    pr = jnp.exp(logit - m)
    po = jnp.exp(s_own - m)
    inv = 1.0 / (jnp.sum(pr, axis=-1, keepdims=True) + jnp.sum(po, axis=-1, keepdims=True))
    p_ref[0] = (pr * inv).astype(BF16)
    oo_ref[0] = _dot((po * inv).astype(BF16), vn)


def _dec_select(scores, kmp, q, kn, vn, bias_past, bias_own, expand, *, n_past):
    b, tq, _ = q.shape
    rows = N_HEADS * tq
    plen = scores.shape[2]
    full = lambda a: pl.BlockSpec(a.shape, lambda i: (0,) * a.ndim)
    per = lambda a: pl.BlockSpec((1,) + a.shape[1:], lambda i: (i,) + (0,) * (a.ndim - 1))
    return pl.pallas_call(
        functools.partial(_dec_select_body, tq=tq, n_past=n_past),
        grid=(b,),
        in_specs=[per(scores), per(kmp), per(q), per(kn), per(vn), full(bias_past), full(bias_own),
                  full(expand)],
        out_specs=(pl.BlockSpec((1, rows, plen), lambda i: (i, 0, 0)),
                   pl.BlockSpec((1, rows, WIDTH), lambda i: (i, 0, 0))),
        out_shape=(jax.ShapeDtypeStruct((b, rows, plen), BF16),
                   jax.ShapeDtypeStruct((b, rows, WIDTH), F32)),
        compiler_params=pltpu.CompilerParams(dimension_semantics=("arbitrary",),
                                             vmem_limit_bytes=VMEM_LIMIT),
        name="dec_select",
    )(scores, kmp, q, kn, vn, bias_past, bias_own, expand)


def _dec_pv_body(pt_ref, p_ref, oo_ref, *refs, pps, tq):
    v_refs = refs[:pps]
    o_ref, acc = refs[pps], refs[pps + 1]
    j = pl.program_id(1)

    @pl.when(j == 0)
    def _():
        acc[...] = oo_ref[0]

    tot = acc[...]
    for i in range(pps):
        tot = tot + _dot(p_ref[0, :, i * LANES:(i + 1) * LANES], v_refs[i][0].astype(BF16), _NT)
    acc[...] = tot

    @pl.when(j == pl.num_programs(1) - 1)
    def _():
        masks = _head_masks(WIDTH)
        a = acc[...]
        out = jnp.where(masks[0], a[0:tq], 0.0)
        for h in range(1, N_HEADS):
            out = out + jnp.where(masks[h], a[h * tq:(h + 1) * tq], 0.0)
        o_ref[0] = out


def _dec_pv(pt_flat, probs, o_own, cache, *, npages, pps, tq):
    b, rows, _ = probs.shape
    grid_spec = pltpu.PrefetchScalarGridSpec(
        num_scalar_prefetch=1,
        grid=(b, npages // pps),
        in_specs=[pl.BlockSpec((1, rows, pps * LANES), lambda bi, j, pt: (bi, 0, j)),
                  pl.BlockSpec((1, rows, WIDTH), lambda bi, j, pt: (bi, 0, 0))] + _page_specs(pps, npages),
        out_specs=pl.BlockSpec((1, tq, WIDTH), lambda bi, j, pt: (bi, 0, 0)),
        scratch_shapes=[pltpu.VMEM((rows, WIDTH), F32)],
    )
    return pl.pallas_call(
        functools.partial(_dec_pv_body, pps=pps, tq=tq),
        grid_spec=grid_spec,
        out_shape=jax.ShapeDtypeStruct((b, tq, WIDTH), F32),
        compiler_params=pltpu.CompilerParams(dimension_semantics=("arbitrary", "arbitrary"),
                                             vmem_limit_bytes=VMEM_LIMIT),
        name="dec_pv",
    )(pt_flat, probs, o_own, *([cache] * pps))


def _outproj_body(oa_ref, ob_ref, x_ref, w_ref, g_ref, wr_ref, br_ref, cnt0_ref,
                  x1_ref, xn_ref, gate_ref, exp_ref, cnt_ref, cnt_scr):
    @pl.when(pl.program_id(0) == 0)
    def _():
        cnt_scr[...] = cnt0_ref[...]

    mixed = _dot(oa_ref[...].astype(BF16), w_ref[0:WIDTH, :]) + _dot(ob_ref[...].astype(BF16),
                                                                      w_ref[WIDTH:2 * WIDTH, :])
    x1 = x_ref[...] + mixed
    x1_ref[...] = x1
    ms = jnp.mean(x1 * x1, axis=-1, keepdims=True)
    xn = x1 * lax.rsqrt(ms + RMS_EPS) * g_ref[...]
    xn_ref[...] = xn.astype(BF16)
    logits = _mm3(xn, wr_ref[...]) + br_ref[...]
    col = lax.broadcasted_iota(jnp.int32, logits.shape, 1)
    g = jnp.where(col < N_EXPERTS, logits, -jnp.inf)
    vals, idxs = [], []
    for _ in range(TOP_K):
        mx = jnp.max(g, axis=-1, keepdims=True)
        idx = jnp.min(jnp.where(g == mx, col, jnp.int32(1 << 30)), axis=-1, keepdims=True)
        vals.append(mx)
        idxs.append(idx)
        g = jnp.where(col == idx, -jnp.inf, g)
    es = [jnp.exp(vv - vals[0]) for vv in vals]
    den = es[0]
    for e in es[1:]:
        den = den + e
    tm = logits.shape[0]
    onehot = jnp.zeros(logits.shape, F32)
    for kk in range(TOP_K):
        onehot = onehot + (col == idxs[kk]).astype(F32)
    r_i = lax.broadcasted_iota(jnp.int32, (tm, tm), 0)
    c_i = lax.broadcasted_iota(jnp.int32, (tm, tm), 1)
    prefix = _dot((r_i > c_i).astype(BF16), onehot.astype(BF16)) + cnt_scr[...]
    gates = jnp.zeros(logits.shape, F32)
    experts = jnp.zeros(logits.shape, jnp.int32)
    for kk in range(TOP_K):
        gates = jnp.where(col == kk, es[kk] / den, gates)
        experts = jnp.where(col == kk, idxs[kk], experts)
        rank = jnp.sum(jnp.where(col == idxs[kk], prefix, 0.0), axis=-1, keepdims=True)
        experts = jnp.where(col == TOP_K + kk, rank.astype(jnp.int32), experts)
    gate_ref[...] = gates
    exp_ref[...] = experts
    cnt_scr[...] = cnt_scr[...] + jnp.sum(onehot, axis=0, keepdims=True)
    cnt_ref[...] = cnt_scr[...]


def _outproj(oa, ob, x2d, w_out, g2, wr, br, cnt0):
    n = x2d.shape[0]
    tm = ROW_TILE
    row = lambda w: pl.BlockSpec((tm, w), lambda i: (i, 0))
    full = lambda a: pl.BlockSpec(a.shape, lambda i: (0,) * a.ndim)
    return pl.pallas_call(
        _outproj_body,
        grid=(n // tm,),
        in_specs=[row(WIDTH), row(WIDTH), row(D_MODEL), full(w_out), full(g2), full(wr), full(br), full(cnt0)],
        out_specs=(row(D_MODEL), row(D_MODEL), row(LANES), row(LANES), full(cnt0)),
        out_shape=(jax.ShapeDtypeStruct((n, D_MODEL), F32), jax.ShapeDtypeStruct((n, D_MODEL), BF16),
                   jax.ShapeDtypeStruct((n, LANES), F32), jax.ShapeDtypeStruct((n, LANES), jnp.int32),
                   jax.ShapeDtypeStruct(cnt0.shape, F32)),
        scratch_shapes=[pltpu.VMEM(cnt0.shape, F32)],
        compiler_params=pltpu.CompilerParams(dimension_semantics=("arbitrary",),
                                             vmem_limit_bytes=VMEM_LIMIT),
        name="outproj",
    )(oa, ob, x2d, w_out, g2, wr, br, cnt0)


def _moe_body(be_ref, nv_ref, x_ref, wg_ref, bg_ref, wu_ref, bu_ref, wd_ref, bdn_ref, after_ref, y_ref,
              wg_s, wu_s, wd_s):
    i = pl.program_id(0)
    new_expert = jnp.logical_or(i == 0, be_ref[i] != be_ref[jnp.maximum(i - 1, 0)])

    @pl.when(new_expert)
    def _():
        wg_s[...] = wg_ref[0].astype(BF16)
        wu_s[...] = wu_ref[0].astype(BF16)
        wd_s[...] = wd_ref[0].astype(BF16)

    @pl.when(i < nv_ref[0])
    def _():
        x = x_ref[...]
        gate = jnp.minimum(_dot(x, wg_s[...]) + bg_ref[0], SWIGLU_LIMIT)
        up = jnp.clip(_dot(x, wu_s[...]) + bu_ref[0], -SWIGLU_LIMIT, SWIGLU_LIMIT)
        glu = gate * _sigmoid(SWIGLU_ALPHA * gate)
        hmid = ((up + 1.0) * glu).astype(BF16)
        y_ref[...] = _dot(hmid, wd_s[...]) + bdn_ref[0]

    @pl.when(i >= nv_ref[0])
    def _():
        y_ref[...] = jnp.zeros_like(y_ref)


def _moe(block_e, nvalid, xb, wg, bg, wu, bu, wd, bdn, after, *, rows):
    n_rows = xb.shape[0]
    nb = n_rows // rows
    d_ff = wg.shape[2]
    wspec = lambda s: pl.BlockSpec((1,) + s, lambda i, be, nv: (be[i], 0, 0))
    grid_spec = pltpu.PrefetchScalarGridSpec(
        num_scalar_prefetch=2,
        grid=(nb,),
        in_specs=[pl.BlockSpec((rows, D_MODEL), lambda i, be, nv: (i, 0)),
                  wspec((D_MODEL, d_ff)), wspec((1, d_ff)), wspec((D_MODEL, d_ff)), wspec((1, d_ff)),
                  wspec((d_ff, D_MODEL)), wspec((1, D_MODEL)),
                  pl.BlockSpec(after.shape, lambda i, be, nv: (0, 0))],
        out_specs=pl.BlockSpec((rows, D_MODEL), lambda i, be, nv: (i, 0)),
        scratch_shapes=[pltpu.VMEM((D_MODEL, d_ff), BF16), pltpu.VMEM((D_MODEL, d_ff), BF16),
                        pltpu.VMEM((d_ff, D_MODEL), BF16)],
    )
    return pl.pallas_call(
        _moe_body,
        grid_spec=grid_spec,
        out_shape=jax.ShapeDtypeStruct((n_rows, D_MODEL), F32),
        compiler_params=pltpu.CompilerParams(dimension_semantics=("arbitrary",),
                                             vmem_limit_bytes=VMEM_LIMIT),
        name="moe_ffn",
    )(block_e, nvalid, xb, wg, bg, wu, bu, wd, bdn, after)


def _blockdiag_state(s):
    b = s.shape[0]
    ng = N_HEADS // GROUP
    s5 = s.reshape(b, ng, GROUP, HEAD_DIM, HEAD_DIM)
    eye = jnp.eye(GROUP, dtype=s.dtype)
    return jnp.einsum("bghde,hk->bghdke", s5, eye).reshape(b, ng, GW, GW)


def _unblock_state(sbd):
    b = sbd.shape[0]
    ng = N_HEADS // GROUP
    s6 = sbd.reshape(b, ng, GROUP, HEAD_DIM, GROUP, HEAD_DIM)
    d = jnp.diagonal(s6, axis1=2, axis2=4)
    return jnp.moveaxis(d, -1, 2).reshape(b, N_HEADS, HEAD_DIM, HEAD_DIM)


def _mixer(x, cache_k, cache_v, page_table, s0, cbuf, rel_bias, lw):
    (g1, w_cat, bd, qg, kg, cw, av, dtb, eb, eg, og) = lw
    b, t, _ = x.shape
    x2d = x.reshape(b * t, D_MODEL)
    prompt = cache_k is None
    outs = _inproj(x2d, g1, w_cat, bd, qg, kg, seq_len=t, attn_layout=prompt)
    qkva, z, bg, qb, kb, vb = outs[:6]
    r3 = lambda a: a.reshape(b, t, a.shape[-1])
    qkva3 = r3(qkva)
    c = min(DELTA_CHUNK, t)
    tt = min(GDN_TILE, t)
    o_a, s_fin = _gdn(qkva3, r3(z), r3(bg), cbuf, _blockdiag_state(s0), cw, av, dtb, eb, eg, bd, og,
                      tt=tt, c=c)
    new_conv = jnp.concatenate([cbuf, qkva3], axis=1)[:, t:] if t < CONV_WIDTH - 1 else qkva3[:, t - (CONV_WIDTH - 1):]
    qb3 = r3(qb)
    rb = rel_bias.astype(F32)

    if prompt:
        kbf, vt, km = outs[6:]
        heads_last = lambda a: jnp.transpose(a.reshape(b, N_HEADS, HEAD_DIM, t), (0, 3, 1, 2))
        k4, v4 = heads_last(kb), heads_last(vb)
        nb = t // MOBA_BLOCK
        kmp = jnp.pad(km.reshape(b, nb, WIDTH), ((0, 0), (0, LANES - nb), (0, 0)))
        ii = np.arange(MOBA_BLOCK)
        d_own = ii[None, :] - ii[:, None]
        far_b = _bucket_np(np.arange(MOBA_BLOCK + 1, max(t, MOBA_BLOCK + 2)))
        assert (far_b == far_b[0]).all()
        bias_tab = _bias_table(rel_bias, np.concatenate([d_own, d_own + MOBA_BLOCK], axis=0))
        bias_tab = bias_tab.reshape(N_HEADS, 2, MOBA_BLOCK, MOBA_BLOCK)
        far = rb[int(far_b[0])]
        o_b = _moba_prompt(qb3, r3(kbf), vt, kmp, bias_tab, far).reshape(b * t, WIDTH)
    else:
        npages = page_table.shape[1]
        page = cache_k.shape[1]
        past = npages * page
        assert page == LANES and past % MOBA_BLOCK == 0 and t <= LANES
        n_past = past // MOBA_BLOCK
        pps = next(n for n in (32, 16, 8, 4, 2) if npages % n == 0)
        rows = N_HEADS * t
        pt_flat = page_table.reshape(-1).astype(jnp.int32)
        slab = lambda cch: jnp.transpose(cch, (0, 2, 3, 1)).reshape(cch.shape[0], WIDTH, page)
        kb3, vb3 = r3(kb), r3(vb)
        k4, v4 = (a.reshape(b, t, N_HEADS, HEAD_DIM) for a in (kb3, vb3))
        scores, kmt = _dec_scores(pt_flat, qb3, slab(cache_k), npages=npages, pps=pps)
        tpos = np.arange(t)
        d_past = past + tpos[:, None] - np.arange(past)[None, :]
        bias_past = _bias_table(rel_bias, d_past).reshape(rows, past)
        d_own = np.zeros((t, LANES), np.int64)
        d_own[:, :t] = tpos[:, None] - tpos[None, :]
        bias_own = _bias_table(rel_bias, d_own).reshape(rows, LANES)
        ex = np.zeros((LANES, past), np.float32)
        ex[np.arange(past) // MOBA_BLOCK, np.arange(past)] = 1.0
        probs, o_own = _dec_select(scores, kmt, qb3, kb3, vb3, bias_past, bias_own, jnp.asarray(ex, BF16),
                                   n_past=n_past)
        o_b = _dec_pv(pt_flat, probs, o_own, slab(cache_v), npages=npages, pps=pps, tq=t).reshape(b * t, WIDTH)
    marker = None if prompt else scores[0, :SUBLANES, :LANES]
    return o_a.reshape(b * t, WIDTH), o_b, k4, v4, _unblock_state(s_fin), new_conv, marker


def _moe_ffn(xn, gates, er, counts, x1, rows, after, wg, bg, wu, bu, wd, bdn):
    n = xn.shape[0]
    nk = n * TOP_K
    flat_e = er[:, :TOP_K].reshape(nk)
    rank = er[:, TOP_K:2 * TOP_K].reshape(nk)
    counts = counts[0, :N_EXPERTS].astype(jnp.int32)
    padded = ((counts + rows - 1) // rows) * rows
    pad_end = jnp.cumsum(padded)
    pad_start = pad_end - padded
    dest = pad_start[flat_e] + rank
    n_blocks = -(-nk // rows) + N_EXPERTS
    n_rows = n_blocks * rows
    flat_tok = jnp.arange(nk, dtype=jnp.int32) // TOP_K
    slot_tok = jnp.zeros((n_rows,), jnp.int32).at[dest].set(flat_tok, unique_indices=True)
    nvalid = (pad_end[-1] // rows).astype(jnp.int32).reshape(1)
    blk_start = jnp.arange(n_blocks, dtype=jnp.int32) * rows
    block_e = jnp.sum((blk_start[:, None] >= pad_end[None, :]).astype(jnp.int32), axis=1)
    block_e = jnp.minimum(block_e, N_EXPERTS - 1)
    last_e = block_e[jnp.maximum(nvalid[0] - 1, 0)]
    block_e = jnp.where(jnp.arange(n_blocks) < nvalid[0], block_e, last_e)
    yb = _moe(block_e, nvalid, xn[slot_tok], wg, bg, wu, bu, wd, bdn, after, rows=rows)
    dest2 = dest.reshape(n, TOP_K)
    y = x1
    for kk in range(TOP_K):
        y = y + gates[:, kk:kk + 1] * yb[dest2[:, kk]]
    return y


def kernel(x_prompt, x_sample, cache_k, cache_v, state_delta, state_conv, page_table, rel_bias,
           norm1_g, w_in, conv_w, A_log, dt_bias, o_norm_g, q_norm_g, k_norm_g, w_out, norm2_g,
           w_router, b_router, w_gate, b_gate, w_up, b_up, w_down, b_down):
    depth = norm1_g.shape[0]
    bp, tp, _ = x_prompt.shape
    bs, ts, _ = x_sample.shape
    yp, ys = x_prompt, x_sample
    outs = [[] for _ in range(8)]
    seg = np.arange(WIDTH) // HEAD_DIM
    segw = np.arange(GW) // HEAD_DIM
    bd = jnp.asarray((segw[:, None] == segw[None, :]).astype(np.float32) / HEAD_DIM, BF16)
    eb_np = np.zeros((LANES, WIDTH), np.float32)
    eb_np[seg, np.arange(WIDTH)] = 1.0
    eg_np = np.zeros((LANES, WIDTH), np.float32)
    eg_np[N_HEADS + seg, np.arange(WIDTH)] = 1.0
    eb, eg = jnp.asarray(eb_np, BF16), jnp.asarray(eg_np, BF16)
    c1, c3 = QKV + WIDTH, QKV + WIDTH + 2 * N_HEADS
    for l in range(depth):
        wl = w_in[l]
        w_cat = jnp.concatenate([wl[:, :c1], wl[:, c3:], wl[:, c1:c3],
                                 jnp.zeros((D_MODEL, LANES - 2 * N_HEADS), wl.dtype)], axis=1).astype(BF16)
        tile8 = lambda g: jnp.tile(g.astype(F32), N_HEADS).reshape(1, WIDTH)
        av = jnp.zeros((1, LANES), F32).at[0, N_HEADS:2 * N_HEADS].set(-jnp.exp(A_log[l].astype(F32)))
        dtb = jnp.zeros((1, LANES), F32).at[0, N_HEADS:2 * N_HEADS].set(dt_bias[l].astype(F32))
        lw = (norm1_g[l].astype(F32).reshape(1, D_MODEL), w_cat, bd, tile8(q_norm_g[l]), tile8(k_norm_g[l]),
              conv_w[l].astype(F32), av, dtb, eb, eg, tile8(o_norm_g[l]))
        w_o = w_out[l].astype(BF16)
        g2 = norm2_g[l].astype(F32).reshape(1, D_MODEL)
        wr = jnp.pad(w_router[l].astype(F32), ((0, 0), (0, LANES - N_EXPERTS)))
        br = jnp.pad(b_router[l].astype(F32), (0, LANES - N_EXPERTS)).reshape(1, LANES)
        cnt0 = jnp.zeros((1, LANES), F32)
        ffn_w = (w_gate[l].astype(F32), b_gate[l].astype(F32)[:, None, :],
                 w_up[l].astype(F32), b_up[l].astype(F32)[:, None, :],
                 w_down[l].astype(F32), b_down[l].astype(F32)[:, None, :])

        def ffn(oa, ob, x, after):
            n = x.shape[0] * x.shape[1]
            x1, xn, gt, er, cnt = _outproj(oa, ob, x.reshape(n, D_MODEL), w_o, g2, wr, br, cnt0)
            rows = MOE_ROWS if n * TOP_K >= 2 * N_EXPERTS * MOE_ROWS else MOE_ROWS_SMALL
            return _moe_ffn(xn, gt, er, cnt, x1, rows, after, *ffn_w).reshape(x.shape)

        s0_p = jnp.zeros((bp, N_HEADS, HEAD_DIM, HEAD_DIM), F32)
        c0_p = jnp.zeros((bp, CONV_WIDTH - 1, QKV), F32)
        oa_p, ob_p, k_p, v_p, s_p, c_p, _ = _mixer(yp, None, None, None, s0_p, c0_p, rel_bias, lw)
        oa_s, ob_s, k_s, v_s, s_s, c_s, dec_marker = _mixer(ys, cache_k[l], cache_v[l], page_table,
                                                            state_delta[l], state_conv[l], rel_bias, lw)
        yp = ffn(oa_p, ob_p, yp, dec_marker)
        ys = ffn(oa_s, ob_s, ys, jnp.zeros((SUBLANES, LANES), F32))
        for lst, val in zip(outs, (k_p, v_p, s_p, c_p, k_s, v_s, s_s, c_s)):
            lst.append(val)
    stacked = [jnp.stack(o) for o in outs]
    return (yp, ys, *stacked)
```
